```python
import jax, jax.numpy as jnp
from jax import lax
import numpy as np

D_MODEL = 2048
BATCH = 1
SEQ = 16384
DEPTH = 4

D_MIX = D_MODEL
D_POOL = D_MIX // 2
D_SGU = D_MIX - D_POOL
POOL_WINDOWS = (2, 4, 8, 16)
N_POOL_GROUPS = len(POOL_WINDOWS)
POOL_GROUP = D_POOL // N_POOL_GROUPS
CHUNK = 128
N_SGU_HEADS = 8
SGU_HEAD = D_SGU // N_SGU_HEADS
D_IN = D_POOL + 2 * D_SGU
D_FF = D_MODEL * 7 // 2
N_EXPERTS = 8
TOP_K = 2
MOE_BLOCK = 256
N_DENSE = (DEPTH + 1) // 2
N_MOE = DEPTH // 2
RMS_EPS = 1e-6
LN_EPS = 1e-5

kernel_name = "hybrid_pool_sgu_moe_encoder"


def rmsnorm(x, g):
    xf = x.astype(jnp.float32)
    y = xf * lax.rsqrt(jnp.mean(xf * xf, axis=-1, keepdims=True) + RMS_EPS)
    return (y * g.astype(jnp.float32)).astype(x.dtype)


def pool_mixer(p, pool_w, pool_scale):
    B, S, _ = p.shape
    pf = p.astype(jnp.float32)
    cs = jnp.concatenate([jnp.zeros((B, 1, D_POOL), jnp.float32), jnp.cumsum(pf, axis=1)], axis=1)
    t = jnp.arange(S)
    diffs = []
    for g, w in enumerate(POOL_WINDOWS):
        sl = slice(g * POOL_GROUP, (g + 1) * POOL_GROUP)
        lo = jnp.clip(t - w // 2, 0, S)
        hi = jnp.clip(t + w - w // 2, 0, S)
        csg = cs[:, :, sl]
        win_sum = jnp.take(csg, hi, axis=1) - jnp.take(csg, lo, axis=1)
        cnt = (hi - lo).astype(jnp.float32)[None, :, None]
        diffs.append(win_sum / cnt - pf[:, :, sl])
    d = jnp.stack(diffs, axis=2).astype(p.dtype)
    y = jnp.einsum('bsgc,gcd->bsgd', d, pool_w).reshape(B, S, D_POOL)
    return y * pool_scale


def sgu_mixer(uv, norm_g, w_s, b_s):
    B, S, _ = uv.shape
    u, v = jnp.split(uv, 2, axis=-1)
    vf = v.astype(jnp.float32)
    mu = jnp.mean(vf, axis=-1, keepdims=True)
    var = jnp.mean(jnp.square(vf - mu), axis=-1, keepdims=True)
    vn = ((vf - mu) * lax.rsqrt(var + LN_EPS) * norm_g.astype(jnp.float32)).astype(v.dtype)
    vc = vn.reshape(B, S // CHUNK, CHUNK, N_SGU_HEADS, SGU_HEAD)
    mixed = jnp.einsum('hqk,bnkhc->bnqhc', w_s, vc) + b_s.T[None, None, :, :, None]
    return u * mixed.reshape(B, S, D_SGU)


def swiglu(h, w_gate, w_up, w_down):
    return (jax.nn.silu(h @ w_gate) * (h @ w_up)) @ w_down


def moe_swiglu(h, router_w, w_gate, w_up, w_down):
    B, S, D = h.shape
    T = B * S
    xt = h.reshape(T, D)
    logits = xt.astype(jnp.float32) @ router_w.astype(jnp.float32)
    top_v, top_i = lax.top_k(logits, TOP_K)
    gates = jax.nn.softmax(top_v, axis=-1)
    A = T * TOP_K
    e_flat = top_i.reshape(A)
    tok_flat = jnp.repeat(jnp.arange(T, dtype=jnp.int32), TOP_K)
    g_flat = gates.reshape(A)
    order = jnp.argsort(e_flat, stable=True)
    e_sorted = e_flat[order]
    counts = jnp.bincount(e_flat, length=N_EXPERTS)
    starts = jnp.cumsum(counts) - counts
    padded = ((counts + MOE_BLOCK - 1) // MOE_BLOCK) * MOE_BLOCK
    pstarts = jnp.cumsum(padded) - padded
    pends = pstarts + padded
    dest = pstarts[e_sorted] + (jnp.arange(A) - starts[e_sorted])
    n_blocks = -(-(A + N_EXPERTS * (MOE_BLOCK - 1)) // MOE_BLOCK)
    P = n_blocks * MOE_BLOCK
    slot_tok = jnp.zeros((P,), jnp.int32).at[dest].set(tok_flat[order])
    slot_gate = jnp.zeros((P,), jnp.float32).at[dest].set(g_flat[order])
    block_expert = jnp.clip(
        jnp.searchsorted(pends, jnp.arange(n_blocks) * MOE_BLOCK, side='right'), 0, N_EXPERTS - 1)
    xs = xt[slot_tok].reshape(n_blocks, MOE_BLOCK, D)

    def run_block(args):
        xb, e = args
        return swiglu(xb, w_gate[e], w_up[e], w_down[e])

    ys = lax.map(run_block, (xs, block_expert)).reshape(P, D)
    out = jnp.zeros((T, D), jnp.float32).at[slot_tok].add(ys.astype(jnp.float32) * slot_gate[:, None])
    return out.astype(h.dtype).reshape(B, S, D)


def setup_inputs(seed: int = 0) -> dict:
    key = jax.random.key(seed)
    ks = jax.random.split(key, 20)
    f32 = jnp.float32

    def nrm(k, shape, scale):
        return jax.random.normal(k, shape, f32) * scale

    def gain(k, shape):
        return 1.0 + 0.02 * jax.random.normal(k, shape, f32)

    return {
        "x": jax.random.normal(ks[0], (BATCH, SEQ, D_MODEL), f32),
        "norm_mix_g": gain(ks[1], (DEPTH, D_MODEL)),
        "w_in": nrm(ks[2], (DEPTH, D_MODEL, D_IN), D_MODEL ** -0.5),
        "pool_w": nrm(ks[3], (DEPTH, N_POOL_GROUPS, POOL_GROUP, POOL_GROUP), POOL_GROUP ** -0.5),
        "pool_scale": gain(ks[4], (DEPTH, D_POOL)),
        "sgu_norm_g": gain(ks[5], (DEPTH, D_SGU)),
        "sgu_w": nrm(ks[6], (DEPTH, N_SGU_HEADS, CHUNK, CHUNK), CHUNK ** -0.5),
        "sgu_b": gain(ks[7], (DEPTH, N_SGU_HEADS, CHUNK)),
        "w_out": nrm(ks[8], (DEPTH, D_MIX, D_MODEL), D_MIX ** -0.5),
        "norm_ffn_g": gain(ks[9], (DEPTH, D_MODEL)),
        "ffn_w_gate": nrm(ks[10], (N_DENSE, D_MODEL, D_FF), D_MODEL ** -0.5),
        "ffn_w_up": nrm(ks[11], (N_DENSE, D_MODEL, D_FF), D_MODEL ** -0.5),
        "ffn_w_down": nrm(ks[12], (N_DENSE, D_FF, D_MODEL), D_FF ** -0.5),
        "router_w": nrm(ks[13], (N_MOE, D_MODEL, N_EXPERTS), D_MODEL ** -0.5),
        "moe_w_gate": nrm(ks[14], (N_MOE, N_EXPERTS, D_MODEL, D_FF), D_MODEL ** -0.5),
        "moe_w_up": nrm(ks[15], (N_MOE, N_EXPERTS, D_MODEL, D_FF), D_MODEL ** -0.5),
        "moe_w_down": nrm(ks[16], (N_MOE, N_EXPERTS, D_FF, D_MODEL), D_FF ** -0.5),
        "final_norm_g": gain(ks[17], (D_MODEL,)),
    }


def reference(x, norm_mix_g, w_in, pool_w, pool_scale, sgu_norm_g, sgu_w, sgu_b, w_out,
              norm_ffn_g, ffn_w_gate, ffn_w_up, ffn_w_down, router_w,
              moe_w_gate, moe_w_up, moe_w_down, final_norm_g):
    for l in range(DEPTH):
        h = rmsnorm(x, norm_mix_g[l])
        z = h @ w_in[l]
        y_pool = pool_mixer(z[..., :D_POOL], pool_w[l], pool_scale[l])
        y_sgu = sgu_mixer(jax.nn.gelu(z[..., D_POOL:], approximate=False),
                          sgu_norm_g[l], sgu_w[l], sgu_b[l])
        x = x + jnp.concatenate([y_pool, y_sgu], axis=-1) @ w_out[l]
        h = rmsnorm(x, norm_ffn_g[l])
        i = l // 2
        if l % 2 == 0:
            x = x + swiglu(h, ffn_w_gate[i], ffn_w_up[i], ffn_w_down[i])
        else:
            x = x + moe_swiglu(h, router_w[i], moe_w_gate[i], moe_w_up[i], moe_w_down[i])
    return rmsnorm(x, final_norm_g)
```

```python
import functools

import jax
import jax.numpy as jnp
from jax import lax
from jax.experimental import pallas as pl
from jax.experimental.pallas import tpu as pltpu

RMS_EPS = 1e-6
LN_EPS = 1e-5
POOL_WINDOWS = (2, 4, 8, 16)
SGU_CHUNK = 128
TOP_K = 2

V7X_LANES = 128
V7X_SUBLANES = 8
V7X_VMEM_BYTES = 64 * 1024 * 1024
HALO = max(POOL_WINDOWS) // 2
assert HALO == V7X_SUBLANES

BF16 = jnp.bfloat16
F32 = jnp.float32


def _vmem_limit(nbytes):
    return int(min(nbytes, V7X_VMEM_BYTES - 6 * 1024 * 1024))


def _rmsnorm(x, g):
    return x * lax.rsqrt(jnp.mean(x * x, axis=-1, keepdims=True) + RMS_EPS) * g


def _gelu(x):
    return 0.5 * x * (1.0 + lax.erf(x * (2.0 ** -0.5)))


def _silu(x):
    return x / (1.0 + jnp.exp(-x))


def _in_proj_body(x_ref, g_ref, w_ref, z_ref):
    h = _rmsnorm(x_ref[...], g_ref[...]).astype(BF16)
    z_ref[...] = jnp.dot(h, w_ref[...], preferred_element_type=F32)


def _in_proj(x, g, w, *, bm):
    t, d = x.shape
    d_in = w.shape[1]
    return pl.pallas_call(
        _in_proj_body,
        grid=(t // bm,),
        in_specs=[
            pl.BlockSpec((bm, d), lambda i: (i, 0)),
            pl.BlockSpec((1, d), lambda i: (0, 0)),
            pl.BlockSpec((d, d_in), lambda i: (0, 0)),
        ],
        out_specs=pl.BlockSpec((bm, d_in), lambda i: (i, 0)),
        out_shape=jax.ShapeDtypeStruct((t, d_in), F32),
        compiler_params=pltpu.CompilerParams(
            dimension_semantics=("arbitrary",),
            vmem_limit_bytes=_vmem_limit(2 * (bm * d * 4 + d * d_in * 2 + bm * d_in * 4) + 16 * 2**20)),
        name="in_proj",
    )(x, g, w)


def _mixer_body(z_ref, zp_ref, zn_ref, x_ref, pw_ref, ps_ref, ng_ref, sw_ref, sb_ref, wo_ref,
                o_ref, ext_ref, vn_ref, y_ref, *, seq):
    i = pl.program_id(0)
    bm = z_ref.shape[0]
    d_pool = ps_ref.shape[1]
    d_sgu = ng_ref.shape[1]
    pg = d_pool // len(POOL_WINDOWS)
    n_heads = sw_ref.shape[0]
    hd = d_sgu // n_heads

    ext_ref[0:HALO, :] = jnp.where(i > 0, zp_ref[...], 0.0)
    ext_ref[HALO:HALO + bm, :] = z_ref[:, 0:d_pool]
    ext_ref[HALO + bm:2 * HALO + bm, :] = jnp.where(i < pl.num_programs(0) - 1, zn_ref[...], 0.0)
    t = i * bm + lax.broadcasted_iota(jnp.int32, (bm, 1), 0)
    for g, w in enumerate(POOL_WINDOWS):
        cs = slice(g * pg, (g + 1) * pg)
        first = HALO - w // 2
        acc = ext_ref[first:first + bm, cs]
        for j in range(1, w):
            acc = acc + ext_ref[first + j:first + j + bm, cs]
        cnt = jnp.minimum(t + (w - w // 2), seq) - jnp.maximum(t - w // 2, 0)
        diff = acc * (1.0 / cnt.astype(F32)) - z_ref[:, cs]
        yp = jnp.dot(diff.astype(BF16), pw_ref[g], preferred_element_type=F32) * ps_ref[:, cs]
        y_ref[:, cs] = yp.astype(BF16)

    v = _gelu(z_ref[:, d_pool + d_sgu:d_pool + 2 * d_sgu])
    mu = jnp.mean(v, axis=-1, keepdims=True)
    vc = v - mu
    var = jnp.mean(vc * vc, axis=-1, keepdims=True)
    vn_ref[...] = (vc * lax.rsqrt(var + LN_EPS) * ng_ref[...]).astype(BF16)
    for c in range(bm // SGU_CHUNK):
        rs = slice(c * SGU_CHUNK, (c + 1) * SGU_CHUNK)
        for h in range(n_heads):
            hs = slice(h * hd, (h + 1) * hd)
            mixed = jnp.dot(sw_ref[h], vn_ref[rs, hs], preferred_element_type=F32) + sb_ref[:, hs]
            u = _gelu(z_ref[rs, d_pool + h * hd:d_pool + (h + 1) * hd])
            y_ref[rs, d_pool + h * hd:d_pool + (h + 1) * hd] = (u * mixed).astype(BF16)

    o_ref[...] = x_ref[...] + jnp.dot(y_ref[...], wo_ref[...], preferred_element_type=F32)


def _mixer(z, x, pool_w, pool_scale, norm_g, sgu_w, sgu_bias, w_out, *, bm):
    t, d = x.shape
    d_in = z.shape[1]
    d_pool = pool_scale.shape[1]
    d_sgu = norm_g.shape[1]
    d_mix = d_pool + d_sgu
    nh = t // HALO
    const2 = lambda i: (0, 0)
    const3 = lambda i: (0, 0, 0)
    return pl.pallas_call(
        functools.partial(_mixer_body, seq=t),
        grid=(t // bm,),
        in_specs=[
            pl.BlockSpec((bm, d_in), lambda i: (i, 0)),
            pl.BlockSpec((HALO, d_pool), lambda i: (jnp.maximum(i * (bm // HALO) - 1, 0), 0)),
            pl.BlockSpec((HALO, d_pool), lambda i: (jnp.minimum((i + 1) * (bm // HALO), nh - 1), 0)),
            pl.BlockSpec((bm, d), lambda i: (i, 0)),
            pl.BlockSpec(pool_w.shape, const3),
            pl.BlockSpec((1, d_pool), const2),
            pl.BlockSpec((1, d_sgu), const2),
            pl.BlockSpec(sgu_w.shape, const3),
            pl.BlockSpec(sgu_bias.shape, const2),
            pl.BlockSpec((d_mix, d), const2),
        ],
        out_specs=pl.BlockSpec((bm, d), lambda i: (i, 0)),
        out_shape=jax.ShapeDtypeStruct((t, d), F32),
        scratch_shapes=[
            pltpu.VMEM((bm + 2 * HALO, d_pool), F32),
            pltpu.VMEM((bm, d_sgu), BF16),
            pltpu.VMEM((bm, d_mix), BF16),
        ],
        compiler_params=pltpu.CompilerParams(
            dimension_semantics=("arbitrary",),
            vmem_limit_bytes=_vmem_limit(2 * (bm * d_in * 4 + 2 * bm * d * 4 + d_mix * d * 2) + 20 * 2**20)),
        name="mixer",
    )(z, z, z, x, pool_w, pool_scale, norm_g, sgu_w, sgu_bias, w_out)


def _dense_ffn_body(x_ref, g_ref, wg_ref, wu_ref, wd_ref, o_ref, h_ref):
    @pl.when(pl.program_id(1) == 0)
    def _():
        x = x_ref[...]
        h_ref[...] = _rmsnorm(x, g_ref[...]).astype(BF16)
        o_ref[...] = x

    h = h_ref[...]
    gate = jnp.dot(h, wg_ref[...], preferred_element_type=F32)
    up = jnp.dot(h, wu_ref[...], preferred_element_type=F32)
    act = (_silu(gate) * up).astype(BF16)
    o_ref[...] += jnp.dot(act, wd_ref[...], preferred_element_type=F32)


def _dense_ffn(x, g, w_gate, w_up, w_down, *, bm, tf):
    t, d = x.shape
    d_ff = w_gate.shape[1]
    return pl.pallas_call(
        _dense_ffn_body,
        grid=(t // bm, d_ff // tf),
        in_specs=[
            pl.BlockSpec((bm, d), lambda i, f: (i, 0)),
            pl.BlockSpec((1, d), lambda i, f: (0, 0)),
            pl.BlockSpec((d, tf), lambda i, f: (0, f)),
            pl.BlockSpec((d, tf), lambda i, f: (0, f)),
            pl.BlockSpec((tf, d), lambda i, f: (f, 0)),
        ],
        out_specs=pl.BlockSpec((bm, d), lambda i, f: (i, 0)),
        out_shape=jax.ShapeDtypeStruct((t, d), F32),
        scratch_shapes=[pltpu.VMEM((bm, d), BF16)],
        compiler_params=pltpu.CompilerParams(
            dimension_semantics=("arbitrary", "arbitrary"),
            vmem_limit_bytes=_vmem_limit(2 * (2 * bm * d * 4 + 3 * d * tf * 2) + bm * d * 2
                                         + 3 * bm * tf * 4 + 8 * 2**20)),
        name="dense_ffn",
    )(x, g, w_gate, w_up, w_down)


def _router_body(x_ref, g_ref, rw_ref, hp_ref, ri_ref, rg_ref, cnt_ref, run_ref, *, n_experts):
    i = pl.program_id(0)
    bm, d = x_ref.shape

    @pl.when(i == 0)
    def _():
        run_ref[...] = jnp.zeros_like(run_ref)

    h = _rmsnorm(x_ref[...], g_ref[...])

    bits = pltpu.bitcast(h.astype(BF16).astype(F32), jnp.uint32)
    hp_ref[...] = (bits[:, 0:d // 2] >> 16) | (bits[:, d // 2:d] & jnp.uint32(0xFFFF0000))

    logits = jnp.dot(h, rw_ref[...], preferred_element_type=F32, precision=lax.Precision.HIGHEST)
    lane = lax.broadcasted_iota(jnp.int32, (bm, V7X_LANES), 1)
    neg = jnp.float32(-jnp.inf)
    lg = jnp.where(lane < n_experts, logits, neg)
    m1 = jnp.max(lg, axis=-1, keepdims=True)
    e1 = jnp.min(jnp.where(lg == m1, lane, V7X_LANES), axis=-1, keepdims=True)
    lg2 = jnp.where(lane == e1, neg, lg)
    m2 = jnp.max(lg2, axis=-1, keepdims=True)
    e2 = jnp.min(jnp.where(lg2 == m2, lane, V7X_LANES), axis=-1, keepdims=True)
    ex = jnp.exp(m2 - m1)
    g1 = 1.0 / (1.0 + ex)
    g2 = ex / (1.0 + ex)

    sel = jnp.where((lane == e1) | (lane == e2), 1.0, 0.0)
    r = lax.broadcasted_iota(jnp.int32, (bm, bm), 0)
    c = lax.broadcasted_iota(jnp.int32, (bm, bm), 1)
    tri = jnp.where(c < r, 1.0, 0.0).astype(BF16)
    prefix = jnp.dot(tri, sel.astype(BF16), preferred_element_type=F32) + run_ref[0:1, :]
    rank1 = jnp.sum(jnp.where(lane == e1, prefix, 0.0), axis=-1, keepdims=True).astype(jnp.int32)
    rank2 = jnp.sum(jnp.where(lane == e2, prefix, 0.0), axis=-1, keepdims=True).astype(jnp.int32)
    run_ref[0:1, :] = run_ref[0:1, :] + jnp.sum(sel, axis=0, keepdims=True)

    ri_ref[...] = jnp.where(lane == 0, e1, jnp.where(lane == 1, e2,
                            jnp.where(lane == 2, rank1, jnp.where(lane == 3, rank2, 0))))
    rg_ref[...] = jnp.where(lane == 0, g1, jnp.where(lane == 1, g2, 0.0))
    cnt_ref[...] = run_ref[...].astype(jnp.int32)


def _router(x, g, router_w_padded, *, bm, n_experts):
    t, d = x.shape
    return pl.pallas_call(
        functools.partial(_router_body, n_experts=n_experts),
        grid=(t // bm,),
        in_specs=[
            pl.BlockSpec((bm, d), lambda i: (i, 0)),
            pl.BlockSpec((1, d), lambda i: (0, 0)),
            pl.BlockSpec((d, V7X_LANES), lambda i: (0, 0)),
        ],
        out_specs=[
            pl.BlockSpec((bm, d // 2), lambda i: (i, 0)),
            pl.BlockSpec((bm, V7X_LANES), lambda i: (i, 0)),
            pl.BlockSpec((bm, V7X_LANES), lambda i: (i, 0)),
            pl.BlockSpec((V7X_SUBLANES, V7X_LANES), lambda i: (0, 0)),
        ],
        out_shape=[
            jax.ShapeDtypeStruct((t, d // 2), jnp.uint32),
            jax.ShapeDtypeStruct((t, V7X_LANES), jnp.int32),
            jax.ShapeDtypeStruct((t, V7X_LANES), F32),
            jax.ShapeDtypeStruct((V7X_SUBLANES, V7X_LANES), jnp.int32),
        ],
        scratch_shapes=[pltpu.VMEM((V7X_SUBLANES, V7X_LANES), F32)],
        compiler_params=pltpu.CompilerParams(dimension_semantics=("arbitrary",)),
        name="router",
    )(x, g, router_w_padded)


def _row_copy(src_hbm, dst_ref, sem, src_row, dst_row):
    return pltpu.make_async_copy(src_hbm.at[pl.ds(src_row, 1)], dst_ref.at[pl.ds(dst_row, 1)], sem)


def _expert_body(exp_ref, nv_ref, tok_ref, hp_hbm, wg_ref, wu_ref, wd_ref, o_ref,
                 xw_ref, xs_ref, sem, *, sub):
    s = pl.program_id(0)
    f = pl.program_id(1)
    sb, d = o_ref.shape
    nvalid = nv_ref[s]

    @pl.when(f == 0)
    def _():
        for j in range(sb // sub):
            rows = pl.ds(j * sub, sub)

            @pl.when(j * sub < nvalid)
            def _():
                def issue(r, carry):
                    _row_copy(hp_hbm, xw_ref, sem, tok_ref[j * sub + r], j * sub + r).start()
                    return carry
                lax.fori_loop(0, sub, issue, 0, unroll=8)

                def drain(r, carry):
                    _row_copy(hp_hbm, xw_ref, sem, 0, j * sub + r).wait()
                    return carry
                lax.fori_loop(0, sub, drain, 0, unroll=8)

                w = xw_ref[rows, :]
                lo = pltpu.bitcast(w << 16, F32)
                hi = pltpu.bitcast(w & jnp.uint32(0xFFFF0000), F32)
                xs_ref[rows, 0:d // 2] = lo.astype(BF16)
                xs_ref[rows, d // 2:d] = hi.astype(BF16)

            o_ref[rows, :] = jnp.zeros((sub, d), F32)

    for j in range(sb // sub):
        rows = pl.ds(j * sub, sub)

        @pl.when(j * sub < nvalid)
        def _():
            x = xs_ref[rows, :]
            gate = jnp.dot(x, wg_ref[...], preferred_element_type=F32)
            up = jnp.dot(x, wu_ref[...], preferred_element_type=F32)
            act = (_silu(gate) * up).astype(BF16)
            o_ref[rows, :] += jnp.dot(act, wd_ref[...], preferred_element_type=F32)


def _experts(sb_expert, sb_nvalid, slot_tok, h_packed, w_gate, w_up, w_down, *, sb, sub, tf):
    n_sb = sb_expert.shape[0]
    _, d, d_ff = w_gate.shape
    nf = d_ff // tf

    def f_idx(s, f, nv_ref):
        return jnp.where(nv_ref[s] > 0, f, nf - 1)

    grid_spec = pltpu.PrefetchScalarGridSpec(
        num_scalar_prefetch=2,
        grid=(n_sb, nf),
        in_specs=[
            pl.BlockSpec((sb,), lambda s, f, e, nv: (s,), memory_space=pltpu.SMEM),
            pl.BlockSpec(memory_space=pl.ANY),
            pl.BlockSpec((None, d, tf), lambda s, f, e, nv: (e[s], 0, f_idx(s, f, nv))),
            pl.BlockSpec((None, d, tf), lambda s, f, e, nv: (e[s], 0, f_idx(s, f, nv))),
            pl.BlockSpec((None, tf, d), lambda s, f, e, nv: (e[s], f_idx(s, f, nv), 0)),
        ],
        out_specs=pl.BlockSpec((sb, d), lambda s, f, e, nv: (s, 0)),
        scratch_shapes=[
            pltpu.VMEM((sb, d // 2), jnp.uint32),
            pltpu.VMEM((sb, d), BF16),
            pltpu.SemaphoreType.DMA,
        ],
    )
    return pl.pallas_call(
        functools.partial(_expert_body, sub=sub),
        grid_spec=grid_spec,
        out_shape=jax.ShapeDtypeStruct((n_sb * sb, d), F32),
        compiler_params=pltpu.CompilerParams(
            dimension_semantics=("arbitrary", "arbitrary"),
            vmem_limit_bytes=_vmem_limit(2 * (sb * d * 4 + 3 * d * tf * 2) + sb * d * 4
                                         + 3 * sub * tf * 4 + 10 * 2**20)),
        name="experts",
    )(sb_expert, sb_nvalid, slot_tok, h_packed, w_gate, w_up, w_down)


def _combine_body(dest_ref, rg_ref, x_ref, ys_hbm, *rest, final):
    if final:
        fg_ref, o_ref, y1_ref, y2_ref, sem = rest
    else:
        o_ref, y1_ref, y2_ref, sem = rest
    bt = x_ref.shape[0]

    def issue(r, carry):
        _row_copy(ys_hbm, y1_ref, sem, dest_ref[2 * r], r).start()
        _row_copy(ys_hbm, y2_ref, sem, dest_ref[2 * r + 1], r).start()
        return carry
    lax.fori_loop(0, bt, issue, 0, unroll=8)

    def drain(r, carry):
        _row_copy(ys_hbm, y1_ref, sem, 0, r).wait()
        _row_copy(ys_hbm, y2_ref, sem, 0, r).wait()
        return carry
    lax.fori_loop(0, bt, drain, 0, unroll=8)

    out = x_ref[...] + (rg_ref[:, 0:1] * y1_ref[...] + rg_ref[:, 1:2] * y2_ref[...])
    if final:
        out = _rmsnorm(out, fg_ref[...])
    o_ref[...] = out


def _combine(dest, rg, x, ys, final_g, *, bt):
    t, d = x.shape
    final = final_g is not None
    in_specs = [
        pl.BlockSpec((TOP_K * bt,), lambda i: (i,), memory_space=pltpu.SMEM),
        pl.BlockSpec((bt, V7X_LANES), lambda i: (i, 0)),
        pl.BlockSpec((bt, d), lambda i: (i, 0)),
        pl.BlockSpec(memory_space=pl.ANY),
    ]
    args = [dest, rg, x, ys]
    if final:
        in_specs.append(pl.BlockSpec((1, d), lambda i: (0, 0)))
        args.append(final_g)
    return pl.pallas_call(
        functools.partial(_combine_body, final=final),
        grid=(t // bt,),
        in_specs=in_specs,
        out_specs=pl.BlockSpec((bt, d), lambda i: (i, 0)),
        out_shape=jax.ShapeDtypeStruct((t, d), F32),
        scratch_shapes=[pltpu.VMEM((bt, d), F32), pltpu.VMEM((bt, d), F32), pltpu.SemaphoreType.DMA],
        compiler_params=pltpu.CompilerParams(dimension_semantics=("arbitrary",)),
        name="combine",
    )(*args)


def _routing_tables(ri, cnt, *, n_experts, sb, n_sb):
    t = ri.shape[0]
    experts = ri[:, 0:TOP_K]
    ranks = ri[:, TOP_K:2 * TOP_K]
    counts = cnt[0, 0:n_experts]
    nsb_e = (counts + sb - 1) // sb
    sb_end = jnp.cumsum(nsb_e)
    sb_start = sb_end - nsb_e
    dest = (sb_start * sb)[experts] + ranks
    tok = jnp.broadcast_to(jnp.arange(t, dtype=jnp.int32)[:, None], (t, TOP_K))
    slot_tok = jnp.zeros((n_sb * sb,), jnp.int32).at[dest.reshape(-1)].set(tok.reshape(-1))
    ids = jnp.arange(n_sb, dtype=jnp.int32)
    used = ids < sb_end[-1]
    last_used = jnp.maximum(sb_end[-1] - 1, 0)
    sb_expert = jnp.clip(jnp.searchsorted(sb_end, jnp.where(used, ids, last_used), side="right"),
                         0, n_experts - 1).astype(jnp.int32)
    nvalid = jnp.clip(counts[sb_expert] - (ids - sb_start[sb_expert]) * sb, 0, sb)
    sb_nvalid = jnp.where(used, nvalid, 0).astype(jnp.int32)
    return sb_expert, sb_nvalid, slot_tok, dest.reshape(-1).astype(jnp.int32)


def _block(n, pref, mult):
    if n <= pref:
        return n
    b = (pref // mult) * mult
    while n % b:
        b -= mult
    return b


def _forward(x, norm_mix_g, w_in, pool_w, pool_scale, sgu_norm_g, sgu_w, sgu_b, w_out, norm_ffn_g,
             ffn_w_gate, ffn_w_up, ffn_w_down, router_w, moe_w_gate, moe_w_up, moe_w_down,
             final_norm_g, *, bm_mix=512, bm_ffn=512, tf=512, sb=1024, sub=256, bt=256):
    b, s, d = x.shape
    assert b == 1, "pooling windows and gating chunks are laid out for a single sequence"
    t = b * s
    depth = w_in.shape[0]
    n_experts = router_w.shape[-1]
    n_heads, chunk = sgu_b.shape[1:]
    d_sgu = sgu_norm_g.shape[1]
    assert chunk == SGU_CHUNK and depth % 2 == 0 and n_experts <= V7X_LANES
    bm_mix = _block(t, bm_mix, SGU_CHUNK)
    bm_ffn = _block(t, bm_ffn, V7X_SUBLANES)
    bt = _block(t, bt, V7X_SUBLANES)
    sb = min(sb, t)
    sub = min(sub, sb)
    tf = _block(ffn_w_gate.shape[-1], tf, V7X_LANES)
    n_sb = (t * TOP_K + n_experts * (sb - 1)) // sb

    x = x.reshape(t, d)
    row = lambda a: a.reshape(1, -1)
    for l in range(depth):
        z = _in_proj(x, row(norm_mix_g[l]), w_in[l].astype(BF16), bm=bm_mix)
        bias = jnp.broadcast_to(sgu_b[l].T[:, :, None], (chunk, n_heads, d_sgu // n_heads)).reshape(chunk, d_sgu)
        x = _mixer(z, x, pool_w[l].astype(BF16), row(pool_scale[l]), row(sgu_norm_g[l]),
                   sgu_w[l].astype(BF16), bias, w_out[l].astype(BF16), bm=bm_mix)
        i = l // 2
        if l % 2 == 0:
            x = _dense_ffn(x, row(norm_ffn_g[l]), ffn_w_gate[i].astype(BF16), ffn_w_up[i].astype(BF16),
                           ffn_w_down[i].astype(BF16), bm=bm_ffn, tf=tf)
        else:
            rw = jnp.pad(router_w[i], ((0, 0), (0, V7X_LANES - n_experts)))
            h_packed, ri, rg, cnt = _router(x, row(norm_ffn_g[l]), rw, bm=bm_ffn, n_experts=n_experts)
            sb_expert, sb_nvalid, slot_tok, dest = _routing_tables(ri, cnt, n_experts=n_experts, sb=sb, n_sb=n_sb)
            ys = _experts(sb_expert, sb_nvalid, slot_tok, h_packed, moe_w_gate[i].astype(BF16),
                          moe_w_up[i].astype(BF16), moe_w_down[i].astype(BF16), sb=sb, sub=sub, tf=tf)
            x = _combine(dest, rg, x, ys, row(final_norm_g) if l == depth - 1 else None, bt=bt)
    return x.reshape(b, s, d)


def kernel(x, norm_mix_g, w_in, pool_w, pool_scale, sgu_norm_g, sgu_w, sgu_b, w_out, norm_ffn_g,
           ffn_w_gate, ffn_w_up, ffn_w_down, router_w, moe_w_gate, moe_w_up, moe_w_down, final_norm_g):
    return _forward(x, norm_mix_g, w_in, pool_w, pool_scale, sgu_norm_g, sgu_w, sgu_b, w_out, norm_ffn_g,
                    ffn_w_gate, ffn_w_up, ffn_w_down, router_w, moe_w_gate, moe_w_up, moe_w_down,
                    final_norm_g)
```

```python
import functools

import jax
import jax.numpy as jnp
from jax import lax
from jax.experimental import pallas as pl
from jax.experimental.pallas import tpu as pltpu

RMS_EPS = 1e-6
LN_EPS = 1e-5
POOL_WINDOWS = (2, 4, 8, 16)
SGU_CHUNK = 128
TOP_K = 2

V7X_LANES = 128
V7X_SUBLANES = 8
V7X_VMEM_BYTES = 64 * 1024 * 1024
HALO = max(POOL_WINDOWS) // 2
assert HALO == V7X_SUBLANES

BF16 = jnp.bfloat16
F32 = jnp.float32


def _vmem_limit(nbytes):
    return int(min(nbytes, V7X_VMEM_BYTES - 6 * 1024 * 1024))


def _rmsnorm(x, g):
    return x * lax.rsqrt(jnp.mean(x * x, axis=-1, keepdims=True) + RMS_EPS) * g


def _gelu(x):
    return 0.5 * x * (1.0 + lax.erf(x * (2.0 ** -0.5)))


def _silu(x):
    return x / (1.0 + jnp.exp(-x))


def _in_proj_body(x_ref, g_ref, w_ref, z_ref):
    h = _rmsnorm(x_ref[...], g_ref[...]).astype(BF16)
    z_ref[...] = jnp.dot(h, w_ref[...], preferred_element_type=F32)


def _in_proj(x, g, w, *, bm):
    t, d = x.shape
    d_in = w.shape[1]
    return pl.pallas_call(
        _in_proj_body,
        grid=(t // bm,),
        in_specs=[
            pl.BlockSpec((bm, d), lambda i: (i, 0)),
            pl.BlockSpec((1, d), lambda i: (0, 0)),
            pl.BlockSpec((d, d_in), lambda i: (0, 0)),
        ],
        out_specs=pl.BlockSpec((bm, d_in), lambda i: (i, 0)),
        out_shape=jax.ShapeDtypeStruct((t, d_in), F32),
        compiler_params=pltpu.CompilerParams(
            dimension_semantics=("arbitrary",),
            vmem_limit_bytes=_vmem_limit(2 * (bm * d * 4 + d * d_in * 2 + bm * d_in * 4) + 16 * 2**20)),
        name="in_proj",
    )(x, g, w)


def _mixer_body(z_ref, zp_ref, zn_ref, x_ref, pw_ref, ps_ref, ng_ref, sw_ref, sb_ref, wo_ref,
                o_ref, ext_ref, vn_ref, y_ref, *, seq):
    i = pl.program_id(0)
    bm = z_ref.shape[0]
    d_pool = ps_ref.shape[1]
    d_sgu = ng_ref.shape[1]
    pg = d_pool // len(POOL_WINDOWS)
    n_heads = sw_ref.shape[0]
    hd = d_sgu // n_heads

    ext_ref[0:HALO, :] = jnp.where(i > 0, zp_ref[...], 0.0)
    ext_ref[HALO:HALO + bm, :] = z_ref[:, 0:d_pool]
    ext_ref[HALO + bm:2 * HALO + bm, :] = jnp.where(i < pl.num_programs(0) - 1, zn_ref[...], 0.0)
    t = i * bm + lax.broadcasted_iota(jnp.int32, (bm, 1), 0)
    for g, w in enumerate(POOL_WINDOWS):
        cs = slice(g * pg, (g + 1) * pg)
        first = HALO - w // 2
        acc = ext_ref[first:first + bm, cs]
        for j in range(1, w):
            acc = acc + ext_ref[first + j:first + j + bm, cs]
        cnt = jnp.minimum(t + (w - w // 2), seq) - jnp.maximum(t - w // 2, 0)
        diff = acc * (1.0 / cnt.astype(F32)) - z_ref[:, cs]
        yp = jnp.dot(diff.astype(BF16), pw_ref[g], preferred_element_type=F32) * ps_ref[:, cs]
        y_ref[:, cs] = yp.astype(BF16)

    v = _gelu(z_ref[:, d_pool + d_sgu:d_pool + 2 * d_sgu])
    mu = jnp.mean(v, axis=-1, keepdims=True)
    vc = v - mu
    var = jnp.mean(vc * vc, axis=-1, keepdims=True)
    vn_ref[...] = (vc * lax.rsqrt(var + LN_EPS) * ng_ref[...]).astype(BF16)
    for c in range(bm // SGU_CHUNK):
        rs = slice(c * SGU_CHUNK, (c + 1) * SGU_CHUNK)
        for h in range(n_heads):
            hs = slice(h * hd, (h + 1) * hd)
            mixed = jnp.dot(sw_ref[h], vn_ref[rs, hs], preferred_element_type=F32) + sb_ref[:, hs]
            u = _gelu(z_ref[rs, d_pool + h * hd:d_pool + (h + 1) * hd])
            y_ref[rs, d_pool + h * hd:d_pool + (h + 1) * hd] = (u * mixed).astype(BF16)

    o_ref[...] = x_ref[...] + jnp.dot(y_ref[...], wo_ref[...], preferred_element_type=F32)


def _mixer(z, x, pool_w, pool_scale, norm_g, sgu_w, sgu_bias, w_out, *, bm):
    t, d = x.shape
    d_in = z.shape[1]
    d_pool = pool_scale.shape[1]
    d_sgu = norm_g.shape[1]
    d_mix = d_pool + d_sgu
    nh = t // HALO
    const2 = lambda i: (0, 0)
    const3 = lambda i: (0, 0, 0)
    return pl.pallas_call(
        functools.partial(_mixer_body, seq=t),
        grid=(t // bm,),
        in_specs=[
            pl.BlockSpec((bm, d_in), lambda i: (i, 0)),
            pl.BlockSpec((HALO, d_pool), lambda i: (jnp.maximum(i * (bm // HALO) - 1, 0), 0)),
            pl.BlockSpec((HALO, d_pool), lambda i: (jnp.minimum((i + 1) * (bm // HALO), nh - 1), 0)),
            pl.BlockSpec((bm, d), lambda i: (i, 0)),
            pl.BlockSpec(pool_w.shape, const3),
            pl.BlockSpec((1, d_pool), const2),
            pl.BlockSpec((1, d_sgu), const2),
            pl.BlockSpec(sgu_w.shape, const3),
            pl.BlockSpec(sgu_bias.shape, const2),
            pl.BlockSpec((d_mix, d), const2),
        ],
        out_specs=pl.BlockSpec((bm, d), lambda i: (i, 0)),
        out_shape=jax.ShapeDtypeStruct((t, d), F32),
        scratch_shapes=[
            pltpu.VMEM((bm + 2 * HALO, d_pool), F32),
            pltpu.VMEM((bm, d_sgu), BF16),
            pltpu.VMEM((bm, d_mix), BF16),
        ],
        compiler_params=pltpu.CompilerParams(
            dimension_semantics=("arbitrary",),
            vmem_limit_bytes=_vmem_limit(2 * (bm * d_in * 4 + 2 * bm * d * 4 + d_mix * d * 2) + 20 * 2**20)),
        name="mixer",
    )(z, z, z, x, pool_w, pool_scale, norm_g, sgu_w, sgu_bias, w_out)


def _dense_ffn_body(x_ref, g_ref, wg_ref, wu_ref, wd_ref, o_ref, h_ref):
    @pl.when(pl.program_id(1) == 0)
    def _():
        x = x_ref[...]
        h_ref[...] = _rmsnorm(x, g_ref[...]).astype(BF16)
        o_ref[...] = x

    h = h_ref[...]
    gate = jnp.dot(h, wg_ref[...], preferred_element_type=F32)
    up = jnp.dot(h, wu_ref[...], preferred_element_type=F32)
    act = (_silu(gate) * up).astype(BF16)
    o_ref[...] += jnp.dot(act, wd_ref[...], preferred_element_type=F32)


def _dense_ffn(x, g, w_gate, w_up, w_down, *, bm, tf):
    t, d = x.shape
    d_ff = w_gate.shape[1]
    return pl.pallas_call(
        _dense_ffn_body,
        grid=(t // bm, d_ff // tf),
        in_specs=[
            pl.BlockSpec((bm, d), lambda i, f: (i, 0)),
            pl.BlockSpec((1, d), lambda i, f: (0, 0)),
            pl.BlockSpec((d, tf), lambda i, f: (0, f)),
            pl.BlockSpec((d, tf), lambda i, f: (0, f)),
            pl.BlockSpec((tf, d), lambda i, f: (f, 0)),
        ],
        out_specs=pl.BlockSpec((bm, d), lambda i, f: (i, 0)),
        out_shape=jax.ShapeDtypeStruct((t, d), F32),
        scratch_shapes=[pltpu.VMEM((bm, d), BF16)],
        compiler_params=pltpu.CompilerParams(
            dimension_semantics=("arbitrary", "arbitrary"),
            vmem_limit_bytes=_vmem_limit(2 * (2 * bm * d * 4 + 3 * d * tf * 2) + bm * d * 2
                                         + 3 * bm * tf * 4 + 8 * 2**20)),
        name="dense_ffn",
    )(x, g, w_gate, w_up, w_down)


def _router_body(x_ref, g_ref, rw_ref, hp_ref, ri_ref, rg_ref, cnt_ref, run_ref, *, n_experts):
    i = pl.program_id(0)
    bm, d = x_ref.shape

    @pl.when(i == 0)
    def _():
        run_ref[...] = jnp.zeros_like(run_ref)

    h = _rmsnorm(x_ref[...], g_ref[...])

    bits = pltpu.bitcast(h.astype(BF16).astype(F32), jnp.uint32)
    hp_ref[...] = (bits[:, 0:d // 2] >> 16) | (bits[:, d // 2:d] & jnp.uint32(0xFFFF0000))

    logits = jnp.dot(h, rw_ref[...], preferred_element_type=F32, precision=lax.Precision.HIGHEST)
    lane = lax.broadcasted_iota(jnp.int32, (bm, V7X_LANES), 1)
    neg = jnp.float32(-jnp.inf)
    lg = jnp.where(lane < n_experts, logits, neg)
    m1 = jnp.max(lg, axis=-1, keepdims=True)
    e1 = jnp.min(jnp.where(lg == m1, lane, V7X_LANES), axis=-1, keepdims=True)
    lg2 = jnp.where(lane == e1, neg, lg)
    m2 = jnp.max(lg2, axis=-1, keepdims=True)
    e2 = jnp.min(jnp.where(lg2 == m2, lane, V7X_LANES), axis=-1, keepdims=True)
    ex = jnp.exp(m2 - m1)
    g1 = 1.0 / (1.0 + ex)
    g2 = ex / (1.0 + ex)

    sel = jnp.where((lane == e1) | (lane == e2), 1.0, 0.0)
    r = lax.broadcasted_iota(jnp.int32, (bm, bm), 0)
    c = lax.broadcasted_iota(jnp.int32, (bm, bm), 1)
    tri = jnp.where(c < r, 1.0, 0.0).astype(BF16)
    prefix = jnp.dot(tri, sel.astype(BF16), preferred_element_type=F32) + run_ref[0:1, :]
    rank1 = jnp.sum(jnp.where(lane == e1, prefix, 0.0), axis=-1, keepdims=True).astype(jnp.int32)
    rank2 = jnp.sum(jnp.where(lane == e2, prefix, 0.0), axis=-1, keepdims=True).astype(jnp.int32)
    run_ref[0:1, :] = run_ref[0:1, :] + jnp.sum(sel, axis=0, keepdims=True)

    ri_ref[...] = jnp.where(lane == 0, e1, jnp.where(lane == 1, e2,
                            jnp.where(lane == 2, rank1, jnp.where(lane == 3, rank2, 0))))
    rg_ref[...] = jnp.where(lane == 0, g1, jnp.where(lane == 1, g2, 0.0))
    cnt_ref[...] = run_ref[...].astype(jnp.int32)


def _router(x, g, router_w_padded, *, bm, n_experts):
    t, d = x.shape
    return pl.pallas_call(
        functools.partial(_router_body, n_experts=n_experts),
        grid=(t // bm,),
        in_specs=[
            pl.BlockSpec((bm, d), lambda i: (i, 0)),
            pl.BlockSpec((1, d), lambda i: (0, 0)),
            pl.BlockSpec((d, V7X_LANES), lambda i: (0, 0)),
        ],
        out_specs=[
            pl.BlockSpec((bm, d // 2), lambda i: (i, 0)),
            pl.BlockSpec((bm, V7X_LANES), lambda i: (i, 0)),
            pl.BlockSpec((bm, V7X_LANES), lambda i: (i, 0)),
            pl.BlockSpec((V7X_SUBLANES, V7X_LANES), lambda i: (0, 0)),
        ],
        out_shape=[
            jax.ShapeDtypeStruct((t, d // 2), jnp.uint32),
            jax.ShapeDtypeStruct((t, V7X_LANES), jnp.int32),
            jax.ShapeDtypeStruct((t, V7X_LANES), F32),
            jax.ShapeDtypeStruct((V7X_SUBLANES, V7X_LANES), jnp.int32),
        ],
        scratch_shapes=[pltpu.VMEM((V7X_SUBLANES, V7X_LANES), F32)],
        compiler_params=pltpu.CompilerParams(dimension_semantics=("arbitrary",)),
        name="router",
    )(x, g, router_w_padded)


def _row_copy(src_hbm, dst_ref, sem, src_row, dst_row):
    return pltpu.make_async_copy(src_hbm.at[pl.ds(src_row, 1)], dst_ref.at[pl.ds(dst_row, 1)], sem)


def _expert_body(exp_ref, nv_ref, tok_ref, hp_hbm, wg_ref, wu_ref, wd_ref, ys_hbm,
                 acc_ref, xw_ref, xs_ref, wgb_ref, wub_ref, wdb_ref, gsem, wsem, *, sub, half):
    s = pl.program_id(0)
    f = pl.program_id(1)
    sb, d = acc_ref.shape
    nsub = sb // sub
    nvalid = nv_ref[s]

    def writeback(blk):
        return pltpu.make_async_copy(acc_ref, ys_hbm.at[pl.ds(blk * sb, sb)], wsem)

    @pl.when(f == 0)
    def _():
        for j in range(nsub):
            @pl.when(j * sub < nvalid)
            def _():
                def issue(r, carry):
                    _row_copy(hp_hbm, xw_ref, gsem.at[j], tok_ref[j * sub + r], j * sub + r).start()
                    return carry
                lax.fori_loop(0, sub, issue, 0, unroll=8)

        @pl.when(s > 0)
        def _():
            writeback(s - 1).wait()
        acc_ref[...] = jnp.zeros_like(acc_ref)

        for j in range(nsub):
            rows = pl.ds(j * sub, sub)

            @pl.when(j * sub < nvalid)
            def _():
                def drain(r, carry):
                    _row_copy(hp_hbm, xw_ref, gsem.at[j], 0, j * sub + r).wait()
                    return carry
                lax.fori_loop(0, sub, drain, 0, unroll=8)
                w = xw_ref[rows, :]
                xs_ref[rows, 0:d // 2] = pltpu.bitcast(w << 16, F32).astype(BF16)
                xs_ref[rows, d // 2:d] = pltpu.bitcast(w & jnp.uint32(0xFFFF0000), F32).astype(BF16)

    def cast_weights():
        wgb_ref[...] = wg_ref[...].astype(BF16)
        wub_ref[...] = wu_ref[...].astype(BF16)
        wdb_ref[...] = wd_ref[...].astype(BF16)

    def ffn(rows):
        x = xs_ref[rows, :]
        gate = jnp.dot(x, wgb_ref[...], preferred_element_type=F32)
        up = jnp.dot(x, wub_ref[...], preferred_element_type=F32)
        act = (_silu(gate) * up).astype(BF16)
        acc_ref[rows, :] += jnp.dot(act, wdb_ref[...], preferred_element_type=F32)

    full = nvalid > (nsub - 1) * sub

    @pl.when(full)
    def _():
        cast_weights()
        for j in range(sb // half):
            ffn(pl.ds(j * half, half))

    @pl.when(jnp.logical_and(nvalid > 0, jnp.logical_not(full)))
    def _():
        cast_weights()
        for j in range(nsub - 1):
            @pl.when(j * sub < nvalid)
            def _():
                ffn(pl.ds(j * sub, sub))

    @pl.when(f == pl.num_programs(1) - 1)
    def _():
        writeback(s).start()

        @pl.when(s == pl.num_programs(0) - 1)
        def _():
            writeback(s).wait()


def _experts(sb_expert, sb_nvalid, slot_tok, h_packed, w_gate, w_up, w_down, *, sb, sub, half, tf):
    n_sb = sb_expert.shape[0]
    _, d, d_ff = w_gate.shape
    nf = d_ff // tf

    def f_idx(s, f, nv_ref):
        return jnp.where(nv_ref[s] > 0, f, nf - 1)

    grid_spec = pltpu.PrefetchScalarGridSpec(
        num_scalar_prefetch=2,
        grid=(n_sb, nf),
        in_specs=[
            pl.BlockSpec((sb,), lambda s, f, e, nv: (s,), memory_space=pltpu.SMEM),
            pl.BlockSpec(memory_space=pl.ANY),
            pl.BlockSpec((None, d, tf), lambda s, f, e, nv: (e[s], 0, f_idx(s, f, nv))),
            pl.BlockSpec((None, d, tf), lambda s, f, e, nv: (e[s], 0, f_idx(s, f, nv))),
            pl.BlockSpec((None, tf, d), lambda s, f, e, nv: (e[s], f_idx(s, f, nv), 0)),
        ],
        out_specs=pl.BlockSpec(memory_space=pl.ANY),
        scratch_shapes=[
            pltpu.VMEM((sb, d), F32),
            pltpu.VMEM((sb, d // 2), jnp.uint32),
            pltpu.VMEM((sb, d), BF16),
            pltpu.VMEM((d, tf), BF16),
            pltpu.VMEM((d, tf), BF16),
            pltpu.VMEM((tf, d), BF16),
            pltpu.SemaphoreType.DMA((sb // sub,)),
            pltpu.SemaphoreType.DMA,
        ],
    )
    weight_bytes = 3 * d * tf * 4
    return pl.pallas_call(
        functools.partial(_expert_body, sub=sub, half=half),
        grid_spec=grid_spec,
        out_shape=jax.ShapeDtypeStruct((n_sb * sb, d), F32),
        compiler_params=pltpu.CompilerParams(
            dimension_semantics=("arbitrary", "arbitrary"),
            vmem_limit_bytes=_vmem_limit(2 * weight_bytes + weight_bytes // 2 + sb * d * (4 + 2 + 2)
                                         + 3 * half * tf * 4 + 8 * 2**20)),
        name="experts",
    )(sb_expert, sb_nvalid, slot_tok, h_packed, w_gate, w_up, w_down)


def _combine_body(dest_ref, rg_ref, x_ref, ys_hbm, *rest, final):
    if final:
        fg_ref, o_ref, y1_ref, y2_ref, sem = rest
    else:
        o_ref, y1_ref, y2_ref, sem = rest
    bt = x_ref.shape[0]

    def issue(r, carry):
        _row_copy(ys_hbm, y1_ref, sem, dest_ref[2 * r], r).start()
        _row_copy(ys_hbm, y2_ref, sem, dest_ref[2 * r + 1], r).start()
        return carry
    lax.fori_loop(0, bt, issue, 0, unroll=8)

    def drain(r, carry):
        _row_copy(ys_hbm, y1_ref, sem, 0, r).wait()
        _row_copy(ys_hbm, y2_ref, sem, 0, r).wait()
        return carry
    lax.fori_loop(0, bt, drain, 0, unroll=8)

    out = x_ref[...] + (rg_ref[:, 0:1] * y1_ref[...] + rg_ref[:, 1:2] * y2_ref[...])
    if final:
        out = _rmsnorm(out, fg_ref[...])
    o_ref[...] = out


def _combine(dest, rg, x, ys, final_g, *, bt):
    t, d = x.shape
    final = final_g is not None
    in_specs = [
        pl.BlockSpec((TOP_K * bt,), lambda i: (i,), memory_space=pltpu.SMEM),
        pl.BlockSpec((bt, V7X_LANES), lambda i: (i, 0)),
        pl.BlockSpec((bt, d), lambda i: (i, 0)),
        pl.BlockSpec(memory_space=pl.ANY),
    ]
    args = [dest, rg, x, ys]
    if final:
        in_specs.append(pl.BlockSpec((1, d), lambda i: (0, 0)))
        args.append(final_g)
    return pl.pallas_call(
        functools.partial(_combine_body, final=final),
        grid=(t // bt,),
        in_specs=in_specs,
        out_specs=pl.BlockSpec((bt, d), lambda i: (i, 0)),
        out_shape=jax.ShapeDtypeStruct((t, d), F32),
        scratch_shapes=[pltpu.VMEM((bt, d), F32), pltpu.VMEM((bt, d), F32), pltpu.SemaphoreType.DMA],
        compiler_params=pltpu.CompilerParams(dimension_semantics=("arbitrary",)),
        name="combine",
    )(*args)


def _routing_tables(ri, cnt, *, n_experts, sb, n_sb):
    t = ri.shape[0]
    experts = ri[:, 0:TOP_K]
    ranks = ri[:, TOP_K:2 * TOP_K]
    counts = cnt[0, 0:n_experts]
    nsb_e = (counts + sb - 1) // sb
    sb_end = jnp.cumsum(nsb_e)
    sb_start = sb_end - nsb_e
    dest = (sb_start * sb)[experts] + ranks
    tok = jnp.broadcast_to(jnp.arange(t, dtype=jnp.int32)[:, None], (t, TOP_K))
    slot_tok = jnp.zeros((n_sb * sb,), jnp.int32).at[dest.reshape(-1)].set(tok.reshape(-1))
    ids = jnp.arange(n_sb, dtype=jnp.int32)
    used = ids < sb_end[-1]
    owner_of = jnp.where(used, ids, jnp.maximum(sb_end[-1] - 1, 0))
    sb_expert = jnp.minimum(jnp.sum(sb_end[None, :] <= owner_of[:, None], axis=1), n_experts - 1).astype(jnp.int32)
    nvalid = jnp.clip(counts[sb_expert] - (ids - sb_start[sb_expert]) * sb, 0, sb)
    sb_nvalid = jnp.where(used, nvalid, 0).astype(jnp.int32)
    return sb_expert, sb_nvalid, slot_tok, dest.reshape(-1).astype(jnp.int32)


def _block(n, pref, mult):
    if n <= pref:
        return n
    b = (pref // mult) * mult
    while n % b:
        b -= mult
    return b


def _forward(x, norm_mix_g, w_in, pool_w, pool_scale, sgu_norm_g, sgu_w, sgu_b, w_out, norm_ffn_g,
             ffn_w_gate, ffn_w_up, ffn_w_down, router_w, moe_w_gate, moe_w_up, moe_w_down,
             final_norm_g, *, bm_mix=512, bm_ffn=512, tf=512, sb=1024, sub=256, half=512, bt=256):
    b, s, d = x.shape
    assert b == 1, "pooling windows and gating chunks are laid out for a single sequence"
    t = b * s
    depth = w_in.shape[0]
    n_experts = router_w.shape[-1]
    n_heads, chunk = sgu_b.shape[1:]
    d_sgu = sgu_norm_g.shape[1]
    assert chunk == SGU_CHUNK and depth % 2 == 0 and n_experts <= V7X_LANES
    bm_mix = _block(t, bm_mix, SGU_CHUNK)
    bm_ffn = _block(t, bm_ffn, V7X_SUBLANES)
    bt = _block(t, bt, V7X_SUBLANES)
    sb = min(sb, t)
    sub = min(sub, sb)
    half = min(half, sb)
    tf = _block(ffn_w_gate.shape[-1], tf, V7X_LANES)
    n_sb = (t * TOP_K + n_experts * (sb - 1)) // sb

    x = x.reshape(t, d)
    row = lambda a: a.reshape(1, -1)
    for l in range(depth):
        z = _in_proj(x, row(norm_mix_g[l]), w_in[l].astype(BF16), bm=bm_mix)
        bias = jnp.broadcast_to(sgu_b[l].T[:, :, None], (chunk, n_heads, d_sgu // n_heads)).reshape(chunk, d_sgu)
        x = _mixer(z, x, pool_w[l].astype(BF16), row(pool_scale[l]), row(sgu_norm_g[l]),
                   sgu_w[l].astype(BF16), bias, w_out[l].astype(BF16), bm=bm_mix)
        i = l // 2
        if l % 2 == 0:
            x = _dense_ffn(x, row(norm_ffn_g[l]), ffn_w_gate[i].astype(BF16), ffn_w_up[i].astype(BF16),
                           ffn_w_down[i].astype(BF16), bm=bm_ffn, tf=tf)
        else:
            rw = jnp.pad(router_w[i], ((0, 0), (0, V7X_LANES - n_experts)))
            h_packed, ri, rg, cnt = _router(x, row(norm_ffn_g[l]), rw, bm=bm_ffn, n_experts=n_experts)
            sb_expert, sb_nvalid, slot_tok, dest = _routing_tables(ri, cnt, n_experts=n_experts, sb=sb, n_sb=n_sb)
            ys = _experts(sb_expert, sb_nvalid, slot_tok, h_packed, moe_w_gate[i], moe_w_up[i], moe_w_down[i],
                          sb=sb, sub=sub, half=half, tf=tf)
            x = _combine(dest, rg, x, ys, row(final_norm_g) if l == depth - 1 else None, bt=bt)
    return x.reshape(b, s, d)


def kernel(x, norm_mix_g, w_in, pool_w, pool_scale, sgu_norm_g, sgu_w, sgu_b, w_out, norm_ffn_g,
           ffn_w_gate, ffn_w_up, ffn_w_down, router_w, moe_w_gate, moe_w_up, moe_w_down, final_norm_g):
    return _forward(x, norm_mix_g, w_in, pool_w, pool_scale, sgu_norm_g, sgu_w, sgu_b, w_out, norm_ffn_g,
                    ffn_w_gate, ffn_w_up, ffn_w_down, router_w, moe_w_gate, moe_w_up, moe_w_down,
                    final_norm_g)
```

```python
import functools

import jax
import jax.numpy as jnp
from jax import lax
from jax.experimental import pallas as pl
from jax.experimental.pallas import tpu as pltpu

RMS_EPS = 1e-6
LN_EPS = 1e-5
POOL_WINDOWS = (2, 4, 8, 16)
SGU_CHUNK = 128
TOP_K = 2

V7X_LANES = 128
V7X_SUBLANES = 8
V7X_VMEM_BYTES = 64 * 1024 * 1024
HALO = max(POOL_WINDOWS) // 2
assert HALO == V7X_SUBLANES

BF16 = jnp.bfloat16
F32 = jnp.float32


def _vmem_limit(nbytes):
    return int(min(nbytes, V7X_VMEM_BYTES - 6 * 1024 * 1024))


def _rmsnorm(x, g):
    return x * lax.rsqrt(jnp.mean(x * x, axis=-1, keepdims=True) + RMS_EPS) * g


def _gelu(x):
    return 0.5 * x * (1.0 + lax.erf(x * (2.0 ** -0.5)))


def _silu(x):
    return x / (1.0 + jnp.exp(-x))


def _in_proj_body(x_ref, g_ref, w_ref, z_ref):
    h = _rmsnorm(x_ref[...], g_ref[...]).astype(BF16)
    z_ref[...] = jnp.dot(h, w_ref[...], preferred_element_type=F32)


def _in_proj(x, g, w, *, bm):
    t, d = x.shape
    d_in = w.shape[1]
    return pl.pallas_call(
        _in_proj_body,
        grid=(t // bm,),
        in_specs=[
            pl.BlockSpec((bm, d), lambda i: (i, 0)),
            pl.BlockSpec((1, d), lambda i: (0, 0)),
            pl.BlockSpec((d, d_in), lambda i: (0, 0)),
        ],
        out_specs=pl.BlockSpec((bm, d_in), lambda i: (i, 0)),
        out_shape=jax.ShapeDtypeStruct((t, d_in), F32),
        compiler_params=pltpu.CompilerParams(
            dimension_semantics=("arbitrary",),
            vmem_limit_bytes=_vmem_limit(2 * (bm * d * 4 + d * d_in * 2 + bm * d_in * 4) + 16 * 2**20)),
        name="in_proj",
    )(x, g, w)


def _mixer_body(z_ref, zp_ref, zn_ref, x_ref, pw_ref, ps_ref, ng_ref, sw_ref, sb_ref, wo_ref,
                o_ref, ext_ref, vn_ref, y_ref, *, seq):
    i = pl.program_id(0)
    bm = z_ref.shape[0]
    d_pool = ps_ref.shape[1]
    d_sgu = ng_ref.shape[1]
    pg = d_pool // len(POOL_WINDOWS)
    n_heads = sw_ref.shape[0]
    hd = d_sgu // n_heads

    ext_ref[0:HALO, :] = jnp.where(i > 0, zp_ref[...], 0.0)
    ext_ref[HALO:HALO + bm, :] = z_ref[:, 0:d_pool]
    ext_ref[HALO + bm:2 * HALO + bm, :] = jnp.where(i < pl.num_programs(0) - 1, zn_ref[...], 0.0)
    t = i * bm + lax.broadcasted_iota(jnp.int32, (bm, 1), 0)
    for g, w in enumerate(POOL_WINDOWS):
        cs = slice(g * pg, (g + 1) * pg)
        first = HALO - w // 2
        acc = ext_ref[first:first + bm, cs]
        for j in range(1, w):
            acc = acc + ext_ref[first + j:first + j + bm, cs]
        cnt = jnp.minimum(t + (w - w // 2), seq) - jnp.maximum(t - w // 2, 0)
        diff = acc * (1.0 / cnt.astype(F32)) - z_ref[:, cs]
        yp = jnp.dot(diff.astype(BF16), pw_ref[g], preferred_element_type=F32) * ps_ref[:, cs]
        y_ref[:, cs] = yp.astype(BF16)

    v = _gelu(z_ref[:, d_pool + d_sgu:d_pool + 2 * d_sgu])
    mu = jnp.mean(v, axis=-1, keepdims=True)
    vc = v - mu
    var = jnp.mean(vc * vc, axis=-1, keepdims=True)
    vn_ref[...] = (vc * lax.rsqrt(var + LN_EPS) * ng_ref[...]).astype(BF16)
    for c in range(bm // SGU_CHUNK):
        rs = slice(c * SGU_CHUNK, (c + 1) * SGU_CHUNK)
        for h in range(n_heads):
            hs = slice(h * hd, (h + 1) * hd)
            mixed = jnp.dot(sw_ref[h], vn_ref[rs, hs], preferred_element_type=F32) + sb_ref[:, hs]
            u = _gelu(z_ref[rs, d_pool + h * hd:d_pool + (h + 1) * hd])
            y_ref[rs, d_pool + h * hd:d_pool + (h + 1) * hd] = (u * mixed).astype(BF16)

    o_ref[...] = x_ref[...] + jnp.dot(y_ref[...], wo_ref[...], preferred_element_type=F32)


def _mixer(z, x, pool_w, pool_scale, norm_g, sgu_w, sgu_bias, w_out, *, bm):
    t, d = x.shape
    d_in = z.shape[1]
    d_pool = pool_scale.shape[1]
    d_sgu = norm_g.shape[1]
    d_mix = d_pool + d_sgu
    nh = t // HALO
    const2 = lambda i: (0, 0)
    const3 = lambda i: (0, 0, 0)
    return pl.pallas_call(
        functools.partial(_mixer_body, seq=t),
        grid=(t // bm,),
        in_specs=[
            pl.BlockSpec((bm, d_in), lambda i: (i, 0)),
            pl.BlockSpec((HALO, d_pool), lambda i: (jnp.maximum(i * (bm // HALO) - 1, 0), 0)),
            pl.BlockSpec((HALO, d_pool), lambda i: (jnp.minimum((i + 1) * (bm // HALO), nh - 1), 0)),
            pl.BlockSpec((bm, d), lambda i: (i, 0)),
            pl.BlockSpec(pool_w.shape, const3),
            pl.BlockSpec((1, d_pool), const2),
            pl.BlockSpec((1, d_sgu), const2),
            pl.BlockSpec(sgu_w.shape, const3),
            pl.BlockSpec(sgu_bias.shape, const2),
            pl.BlockSpec((d_mix, d), const2),
        ],
        out_specs=pl.BlockSpec((bm, d), lambda i: (i, 0)),
        out_shape=jax.ShapeDtypeStruct((t, d), F32),
        scratch_shapes=[
            pltpu.VMEM((bm + 2 * HALO, d_pool), F32),
            pltpu.VMEM((bm, d_sgu), BF16),
            pltpu.VMEM((bm, d_mix), BF16),
        ],
        compiler_params=pltpu.CompilerParams(
            dimension_semantics=("arbitrary",),
            vmem_limit_bytes=_vmem_limit(2 * (bm * d_in * 4 + 2 * bm * d * 4 + d_mix * d * 2) + 20 * 2**20)),
        name="mixer",
    )(z, z, z, x, pool_w, pool_scale, norm_g, sgu_w, sgu_bias, w_out)


def _dense_ffn_body(x_ref, g_ref, wg_ref, wu_ref, wd_ref, o_ref, h_ref):
    @pl.when(pl.program_id(1) == 0)
    def _():
        x = x_ref[...]
        h_ref[...] = _rmsnorm(x, g_ref[...]).astype(BF16)
        o_ref[...] = x

    h = h_ref[...]
    gate = jnp.dot(h, wg_ref[...], preferred_element_type=F32)
    up = jnp.dot(h, wu_ref[...], preferred_element_type=F32)
    act = (_silu(gate) * up).astype(BF16)
    o_ref[...] += jnp.dot(act, wd_ref[...], preferred_element_type=F32)


def _dense_ffn(x, g, w_gate, w_up, w_down, *, bm, tf):
    t, d = x.shape
    d_ff = w_gate.shape[1]
    return pl.pallas_call(
        _dense_ffn_body,
        grid=(t // bm, d_ff // tf),
        in_specs=[
            pl.BlockSpec((bm, d), lambda i, f: (i, 0)),
            pl.BlockSpec((1, d), lambda i, f: (0, 0)),
            pl.BlockSpec((d, tf), lambda i, f: (0, f)),
            pl.BlockSpec((d, tf), lambda i, f: (0, f)),
            pl.BlockSpec((tf, d), lambda i, f: (f, 0)),
        ],
        out_specs=pl.BlockSpec((bm, d), lambda i, f: (i, 0)),
        out_shape=jax.ShapeDtypeStruct((t, d), F32),
        scratch_shapes=[pltpu.VMEM((bm, d), BF16)],
        compiler_params=pltpu.CompilerParams(
            dimension_semantics=("arbitrary", "arbitrary"),
            vmem_limit_bytes=_vmem_limit(2 * (2 * bm * d * 4 + 3 * d * tf * 2) + bm * d * 2
                                         + 3 * bm * tf * 4 + 8 * 2**20)),
        name="dense_ffn",
    )(x, g, w_gate, w_up, w_down)


def _router_body(x_ref, g_ref, rw_ref, hp_ref, ri_ref, rg_ref, cnt_ref, run_ref, *, n_experts):
    i = pl.program_id(0)
    bm, d = x_ref.shape

    @pl.when(i == 0)
    def _():
        run_ref[...] = jnp.zeros_like(run_ref)

    h = _rmsnorm(x_ref[...], g_ref[...])

    bits = pltpu.bitcast(h.astype(BF16).astype(F32), jnp.uint32)
    hp_ref[...] = (bits[:, 0:d // 2] >> 16) | (bits[:, d // 2:d] & jnp.uint32(0xFFFF0000))

    logits = jnp.dot(h, rw_ref[...], preferred_element_type=F32, precision=lax.Precision.HIGHEST)
    lane = lax.broadcasted_iota(jnp.int32, (bm, V7X_LANES), 1)
    neg = jnp.float32(-jnp.inf)
    lg = jnp.where(lane < n_experts, logits, neg)
    m1 = jnp.max(lg, axis=-1, keepdims=True)
    e1 = jnp.min(jnp.where(lg == m1, lane, V7X_LANES), axis=-1, keepdims=True)
    lg2 = jnp.where(lane == e1, neg, lg)
    m2 = jnp.max(lg2, axis=-1, keepdims=True)
    e2 = jnp.min(jnp.where(lg2 == m2, lane, V7X_LANES), axis=-1, keepdims=True)
    ex = jnp.exp(m2 - m1)
    g1 = 1.0 / (1.0 + ex)
    g2 = ex / (1.0 + ex)

    sel = jnp.where((lane == e1) | (lane == e2), 1.0, 0.0)
    r = lax.broadcasted_iota(jnp.int32, (bm, bm), 0)
    c = lax.broadcasted_iota(jnp.int32, (bm, bm), 1)
    tri = jnp.where(c < r, 1.0, 0.0).astype(BF16)
    prefix = jnp.dot(tri, sel.astype(BF16), preferred_element_type=F32) + run_ref[0:1, :]
    rank1 = jnp.sum(jnp.where(lane == e1, prefix, 0.0), axis=-1, keepdims=True).astype(jnp.int32)
    rank2 = jnp.sum(jnp.where(lane == e2, prefix, 0.0), axis=-1, keepdims=True).astype(jnp.int32)
    run_ref[0:1, :] = run_ref[0:1, :] + jnp.sum(sel, axis=0, keepdims=True)

    ri_ref[...] = jnp.where(lane == 0, e1, jnp.where(lane == 1, e2,
                            jnp.where(lane == 2, rank1, jnp.where(lane == 3, rank2, 0))))
    rg_ref[...] = jnp.where(lane == 0, g1, jnp.where(lane == 1, g2, 0.0))
    cnt_ref[...] = run_ref[...].astype(jnp.int32)


def _router(x, g, router_w_padded, *, bm, n_experts):
    t, d = x.shape
    return pl.pallas_call(
        functools.partial(_router_body, n_experts=n_experts),
        grid=(t // bm,),
        in_specs=[
            pl.BlockSpec((bm, d), lambda i: (i, 0)),
            pl.BlockSpec((1, d), lambda i: (0, 0)),
            pl.BlockSpec((d, V7X_LANES), lambda i: (0, 0)),
        ],
        out_specs=[
            pl.BlockSpec((bm, d // 2), lambda i: (i, 0)),
            pl.BlockSpec((bm, V7X_LANES), lambda i: (i, 0)),
            pl.BlockSpec((bm, V7X_LANES), lambda i: (i, 0)),
            pl.BlockSpec((V7X_SUBLANES, V7X_LANES), lambda i: (0, 0)),
        ],
        out_shape=[
            jax.ShapeDtypeStruct((t, d // 2), jnp.uint32),
            jax.ShapeDtypeStruct((t, V7X_LANES), jnp.int32),
            jax.ShapeDtypeStruct((t, V7X_LANES), F32),
            jax.ShapeDtypeStruct((V7X_SUBLANES, V7X_LANES), jnp.int32),
        ],
        scratch_shapes=[pltpu.VMEM((V7X_SUBLANES, V7X_LANES), F32)],
        compiler_params=pltpu.CompilerParams(dimension_semantics=("arbitrary",)),
        name="router",
    )(x, g, router_w_padded)


def _row_copy(src_hbm, dst_ref, sem, src_row, dst_row):
    return pltpu.make_async_copy(src_hbm.at[pl.ds(src_row, 1)], dst_ref.at[pl.ds(dst_row, 1)], sem)


def _expert_body(exp_ref, nv_ref, tok_ref, hp_hbm, wg_ref, wu_ref, wd_ref, ys_hbm,
                 acc_ref, xw_ref, xs_ref, wgb_ref, wub_ref, wdb_ref, gsem, wsem, *, sub, half):
    s = pl.program_id(0)
    f = pl.program_id(1)
    sb, d = acc_ref.shape
    nsub = sb // sub
    nvalid = nv_ref[s]

    def writeback(blk):
        return pltpu.make_async_copy(acc_ref, ys_hbm.at[pl.ds(blk * sb, sb)], wsem)

    @pl.when(f == 0)
    def _():
        for j in range(nsub):
            @pl.when(j * sub < nvalid)
            def _():
                def issue(r, carry):
                    _row_copy(hp_hbm, xw_ref, gsem.at[j], tok_ref[j * sub + r], j * sub + r).start()
                    return carry
                lax.fori_loop(0, sub, issue, 0, unroll=8)

        @pl.when(s > 0)
        def _():
            writeback(s - 1).wait()
        acc_ref[...] = jnp.zeros_like(acc_ref)

        for j in range(nsub):
            rows = pl.ds(j * sub, sub)

            @pl.when(j * sub < nvalid)
            def _():
                def drain(r, carry):
                    _row_copy(hp_hbm, xw_ref, gsem.at[j], 0, j * sub + r).wait()
                    return carry
                lax.fori_loop(0, sub, drain, 0, unroll=8)
                w = xw_ref[rows, :]
                xs_ref[rows, 0:d // 2] = pltpu.bitcast(w << 16, F32).astype(BF16)
                xs_ref[rows, d // 2:d] = pltpu.bitcast(w & jnp.uint32(0xFFFF0000), F32).astype(BF16)

    def cast_weights():
        wgb_ref[...] = wg_ref[...].astype(BF16)
        wub_ref[...] = wu_ref[...].astype(BF16)
        wdb_ref[...] = wd_ref[...].astype(BF16)

    def ffn(rows):
        x = xs_ref[rows, :]
        gate = jnp.dot(x, wgb_ref[...], preferred_element_type=F32)
        up = jnp.dot(x, wub_ref[...], preferred_element_type=F32)
        act = (_silu(gate) * up).astype(BF16)
        acc_ref[rows, :] += jnp.dot(act, wdb_ref[...], preferred_element_type=F32)

    full = nvalid > (nsub - 1) * sub

    @pl.when(full)
    def _():
        cast_weights()
        for j in range(sb // half):
            ffn(pl.ds(j * half, half))

    @pl.when(jnp.logical_and(nvalid > 0, jnp.logical_not(full)))
    def _():
        cast_weights()
        for j in range(nsub - 1):
            @pl.when(j * sub < nvalid)
            def _():
                ffn(pl.ds(j * sub, sub))

    @pl.when(f == pl.num_programs(1) - 1)
    def _():
        writeback(s).start()

        @pl.when(s == pl.num_programs(0) - 1)
        def _():
            writeback(s).wait()


def _experts(sb_expert, sb_nvalid, slot_tok, h_packed, w_gate, w_up, w_down, *, layer, sb, sub, half, tf):
    n_sb = sb_expert.shape[0]
    _, _, d, d_ff = w_gate.shape
    nf = d_ff // tf

    def f_idx(s, f, nv_ref):
        return jnp.where(nv_ref[s] > 0, f, nf - 1)

    grid_spec = pltpu.PrefetchScalarGridSpec(
        num_scalar_prefetch=2,
        grid=(n_sb, nf),
        in_specs=[
            pl.BlockSpec((sb,), lambda s, f, e, nv: (s,), memory_space=pltpu.SMEM),
            pl.BlockSpec(memory_space=pl.ANY),
            pl.BlockSpec((None, None, d, tf), lambda s, f, e, nv: (layer, e[s], 0, f_idx(s, f, nv))),
            pl.BlockSpec((None, None, d, tf), lambda s, f, e, nv: (layer, e[s], 0, f_idx(s, f, nv))),
            pl.BlockSpec((None, None, tf, d), lambda s, f, e, nv: (layer, e[s], f_idx(s, f, nv), 0)),
        ],
        out_specs=pl.BlockSpec(memory_space=pl.ANY),
        scratch_shapes=[
            pltpu.VMEM((sb, d), F32),
            pltpu.VMEM((sb, d // 2), jnp.uint32),
            pltpu.VMEM((sb, d), BF16),
            pltpu.VMEM((d, tf), BF16),
            pltpu.VMEM((d, tf), BF16),
            pltpu.VMEM((tf, d), BF16),
            pltpu.SemaphoreType.DMA((sb // sub,)),
            pltpu.SemaphoreType.DMA,
        ],
    )
    weight_bytes = 3 * d * tf * 4
    return pl.pallas_call(
        functools.partial(_expert_body, sub=sub, half=half),
        grid_spec=grid_spec,
        out_shape=jax.ShapeDtypeStruct((n_sb * sb, d), F32),
        compiler_params=pltpu.CompilerParams(
            dimension_semantics=("arbitrary", "arbitrary"),
            vmem_limit_bytes=_vmem_limit(2 * weight_bytes + weight_bytes // 2 + sb * d * (4 + 2 + 2)
                                         + 3 * half * tf * 4 + 8 * 2**20)),
        name="experts",
    )(sb_expert, sb_nvalid, slot_tok, h_packed, w_gate, w_up, w_down)


def _combine_body(dest_ref, rg_ref, x_ref, ys_hbm, *rest, final):
    if final:
        fg_ref, o_ref, y1_ref, y2_ref, sem = rest
    else:
        o_ref, y1_ref, y2_ref, sem = rest
    bt = x_ref.shape[0]

    def issue(r, carry):
        _row_copy(ys_hbm, y1_ref, sem, dest_ref[2 * r], r).start()
        _row_copy(ys_hbm, y2_ref, sem, dest_ref[2 * r + 1], r).start()
        return carry
    lax.fori_loop(0, bt, issue, 0, unroll=8)

    def drain(r, carry):
        _row_copy(ys_hbm, y1_ref, sem, 0, r).wait()
        _row_copy(ys_hbm, y2_ref, sem, 0, r).wait()
        return carry
    lax.fori_loop(0, bt, drain, 0, unroll=8)

    out = x_ref[...] + (rg_ref[:, 0:1] * y1_ref[...] + rg_ref[:, 1:2] * y2_ref[...])
    if final:
        out = _rmsnorm(out, fg_ref[...])
    o_ref[...] = out


def _combine(dest, rg, x, ys, final_g, *, bt):
    t, d = x.shape
    final = final_g is not None
    in_specs = [
        pl.BlockSpec((TOP_K * bt,), lambda i: (i,), memory_space=pltpu.SMEM),
        pl.BlockSpec((bt, V7X_LANES), lambda i: (i, 0)),
        pl.BlockSpec((bt, d), lambda i: (i, 0)),
        pl.BlockSpec(memory_space=pl.ANY),
    ]
    args = [dest, rg, x, ys]
    if final:
        in_specs.append(pl.BlockSpec((1, d), lambda i: (0, 0)))
        args.append(final_g)
    return pl.pallas_call(
        functools.partial(_combine_body, final=final),
        grid=(t // bt,),
        in_specs=in_specs,
        out_specs=pl.BlockSpec((bt, d), lambda i: (i, 0)),
        out_shape=jax.ShapeDtypeStruct((t, d), F32),
        scratch_shapes=[pltpu.VMEM((bt, d), F32), pltpu.VMEM((bt, d), F32), pltpu.SemaphoreType.DMA],
        compiler_params=pltpu.CompilerParams(dimension_semantics=("arbitrary",)),
        name="combine",
    )(*args)


def _routing_tables(ri, cnt, *, n_experts, sb, n_sb):
    t = ri.shape[0]
    experts = ri[:, 0:TOP_K]
    ranks = ri[:, TOP_K:2 * TOP_K]
    counts = cnt[0, 0:n_experts]
    nsb_e = (counts + sb - 1) // sb
    sb_end = jnp.cumsum(nsb_e)
    sb_start = sb_end - nsb_e
    dest = (sb_start * sb)[experts] + ranks
    tok = jnp.broadcast_to(jnp.arange(t, dtype=jnp.int32)[:, None], (t, TOP_K))
    slot_tok = jnp.zeros((n_sb * sb,), jnp.int32).at[dest.reshape(-1)].set(tok.reshape(-1))
    ids = jnp.arange(n_sb, dtype=jnp.int32)
    used = ids < sb_end[-1]
    owner_of = jnp.where(used, ids, jnp.maximum(sb_end[-1] - 1, 0))
    sb_expert = jnp.minimum(jnp.sum(sb_end[None, :] <= owner_of[:, None], axis=1), n_experts - 1).astype(jnp.int32)
    nvalid = jnp.clip(counts[sb_expert] - (ids - sb_start[sb_expert]) * sb, 0, sb)
    sb_nvalid = jnp.where(used, nvalid, 0).astype(jnp.int32)
    return sb_expert, sb_nvalid, slot_tok, dest.reshape(-1).astype(jnp.int32)


def _block(n, pref, mult):
    if n <= pref:
        return n
    b = (pref // mult) * mult
    while n % b:
        b -= mult
    return b


def _forward(x, norm_mix_g, w_in, pool_w, pool_scale, sgu_norm_g, sgu_w, sgu_b, w_out, norm_ffn_g,
             ffn_w_gate, ffn_w_up, ffn_w_down, router_w, moe_w_gate, moe_w_up, moe_w_down,
             final_norm_g, *, bm_mix=512, bm_ffn=512, tf=512, sb=1024, sub=256, half=512, bt=256):
    b, s, d = x.shape
    assert b == 1, "pooling windows and gating chunks are laid out for a single sequence"
    t = b * s
    depth = w_in.shape[0]
    n_experts = router_w.shape[-1]
    n_heads, chunk = sgu_b.shape[1:]
    d_sgu = sgu_norm_g.shape[1]
    assert chunk == SGU_CHUNK and depth % 2 == 0 and n_experts <= V7X_LANES
    bm_mix = _block(t, bm_mix, SGU_CHUNK)
    bm_ffn = _block(t, bm_ffn, V7X_SUBLANES)
    bt = _block(t, bt, V7X_SUBLANES)
    sb = min(sb, t)
    sub = min(sub, sb)
    half = min(half, sb)
    tf = _block(ffn_w_gate.shape[-1], tf, V7X_LANES)
    n_sb = (t * TOP_K + n_experts * (sb - 1)) // sb

    x = x.reshape(t, d)
    row = lambda a: a.reshape(1, -1)
    for l in range(depth):
        z = _in_proj(x, row(norm_mix_g[l]), w_in[l].astype(BF16), bm=bm_mix)
        bias = jnp.broadcast_to(sgu_b[l].T[:, :, None], (chunk, n_heads, d_sgu // n_heads)).reshape(chunk, d_sgu)
        x = _mixer(z, x, pool_w[l].astype(BF16), row(pool_scale[l]), row(sgu_norm_g[l]),
                   sgu_w[l].astype(BF16), bias, w_out[l].astype(BF16), bm=bm_mix)
        i = l // 2
        if l % 2 == 0:
            x = _dense_ffn(x, row(norm_ffn_g[l]), ffn_w_gate[i].astype(BF16), ffn_w_up[i].astype(BF16),
                           ffn_w_down[i].astype(BF16), bm=bm_ffn, tf=tf)
        else:
            rw = jnp.pad(router_w[i], ((0, 0), (0, V7X_LANES - n_experts)))
            h_packed, ri, rg, cnt = _router(x, row(norm_ffn_g[l]), rw, bm=bm_ffn, n_experts=n_experts)
            sb_expert, sb_nvalid, slot_tok, dest = _routing_tables(ri, cnt, n_experts=n_experts, sb=sb, n_sb=n_sb)
            ys = _experts(sb_expert, sb_nvalid, slot_tok, h_packed, moe_w_gate, moe_w_up, moe_w_down,
                          layer=i, sb=sb, sub=sub, half=half, tf=tf)
            x = _combine(dest, rg, x, ys, row(final_norm_g) if l == depth - 1 else None, bt=bt)
    return x.reshape(b, s, d)


def kernel(x, norm_mix_g, w_in, pool_w, pool_scale, sgu_norm_g, sgu_w, sgu_b, w_out, norm_ffn_g,
           ffn_w_gate, ffn_w_up, ffn_w_down, router_w, moe_w_gate, moe_w_up, moe_w_down, final_norm_g):
    return _forward(x, norm_mix_g, w_in, pool_w, pool_scale, sgu_norm_g, sgu_w, sgu_b, w_out, norm_ffn_g,
                    ffn_w_gate, ffn_w_up, ffn_w_down, router_w, moe_w_gate, moe_w_up, moe_w_down,
                    final_norm_g)
```

```python
import functools

import jax
import jax.numpy as jnp
from jax import lax
from jax.experimental import pallas as pl
from jax.experimental.pallas import tpu as pltpu

RMS_EPS = 1e-6
LN_EPS = 1e-5
POOL_WINDOWS = (2, 4, 8, 16)
SGU_CHUNK = 128
TOP_K = 2

V7X_LANES = 128
V7X_SUBLANES = 8
V7X_VMEM_BYTES = 64 * 1024 * 1024
HALO = max(POOL_WINDOWS) // 2
assert HALO == V7X_SUBLANES

BF16 = jnp.bfloat16
F32 = jnp.float32


def _vmem_limit(nbytes):
    return int(min(nbytes, V7X_VMEM_BYTES - 6 * 1024 * 1024))


def _rmsnorm(x, g):
    return x * lax.rsqrt(jnp.mean(x * x, axis=-1, keepdims=True) + RMS_EPS) * g


def _gelu(x):
    return 0.5 * x * (1.0 + lax.erf(x * (2.0 ** -0.5)))


def _silu(x):
    return x / (1.0 + jnp.exp(-x))


def _in_proj_body(x_ref, g_ref, w_ref, z_ref):
    h = _rmsnorm(x_ref[...], g_ref[...]).astype(BF16)
    z_ref[...] = jnp.dot(h, w_ref[...], preferred_element_type=F32)


def _in_proj(x, g, w, *, bm):
    t, d = x.shape
    d_in = w.shape[1]
    return pl.pallas_call(
        _in_proj_body,
        grid=(t // bm,),
        in_specs=[
            pl.BlockSpec((bm, d), lambda i: (i, 0)),
            pl.BlockSpec((1, d), lambda i: (0, 0)),
            pl.BlockSpec((d, d_in), lambda i: (0, 0)),
        ],
        out_specs=pl.BlockSpec((bm, d_in), lambda i: (i, 0)),
        out_shape=jax.ShapeDtypeStruct((t, d_in), F32),
        compiler_params=pltpu.CompilerParams(
            dimension_semantics=("arbitrary",),
            vmem_limit_bytes=_vmem_limit(2 * (bm * d * 4 + d * d_in * 2 + bm * d_in * 4) + 16 * 2**20)),
        name="in_proj",
    )(x, g, w)


def _mixer_body(z_ref, zp_ref, zn_ref, x_ref, pw_ref, ps_ref, ng_ref, sw_ref, sb_ref, wo_ref,
                o_ref, ext_ref, vn_ref, y_ref, *, seq):
    i = pl.program_id(0)
    bm = z_ref.shape[0]
    d_pool = ps_ref.shape[1]
    d_sgu = ng_ref.shape[1]
    pg = d_pool // len(POOL_WINDOWS)
    n_heads = sw_ref.shape[0]
    hd = d_sgu // n_heads

    ext_ref[0:HALO, :] = jnp.where(i > 0, zp_ref[...], 0.0)
    ext_ref[HALO:HALO + bm, :] = z_ref[:, 0:d_pool]
    ext_ref[HALO + bm:2 * HALO + bm, :] = jnp.where(i < pl.num_programs(0) - 1, zn_ref[...], 0.0)
    t = i * bm + lax.broadcasted_iota(jnp.int32, (bm, 1), 0)
    for g, w in enumerate(POOL_WINDOWS):
        cs = slice(g * pg, (g + 1) * pg)
        first = HALO - w // 2
        acc = ext_ref[first:first + bm, cs]
        for j in range(1, w):
            acc = acc + ext_ref[first + j:first + j + bm, cs]
        cnt = jnp.minimum(t + (w - w // 2), seq) - jnp.maximum(t - w // 2, 0)
        diff = acc * (1.0 / cnt.astype(F32)) - z_ref[:, cs]
        yp = jnp.dot(diff.astype(BF16), pw_ref[g], preferred_element_type=F32) * ps_ref[:, cs]
        y_ref[:, cs] = yp.astype(BF16)

    v = _gelu(z_ref[:, d_pool + d_sgu:d_pool + 2 * d_sgu])
    mu = jnp.mean(v, axis=-1, keepdims=True)
    vc = v - mu
    var = jnp.mean(vc * vc, axis=-1, keepdims=True)
    vn_ref[...] = (vc * lax.rsqrt(var + LN_EPS) * ng_ref[...]).astype(BF16)
    for c in range(bm // SGU_CHUNK):
        rs = slice(c * SGU_CHUNK, (c + 1) * SGU_CHUNK)
        for h in range(n_heads):
            hs = slice(h * hd, (h + 1) * hd)
            mixed = jnp.dot(sw_ref[h], vn_ref[rs, hs], preferred_element_type=F32) + sb_ref[:, hs]
            u = _gelu(z_ref[rs, d_pool + h * hd:d_pool + (h + 1) * hd])
            y_ref[rs, d_pool + h * hd:d_pool + (h + 1) * hd] = (u * mixed).astype(BF16)

    o_ref[...] = x_ref[...] + jnp.dot(y_ref[...], wo_ref[...], preferred_element_type=F32)


def _mixer(z, x, pool_w, pool_scale, norm_g, sgu_w, sgu_bias, w_out, *, bm):
    t, d = x.shape
    d_in = z.shape[1]
    d_pool = pool_scale.shape[1]
    d_sgu = norm_g.shape[1]
    d_mix = d_pool + d_sgu
    nh = t // HALO
    const2 = lambda i: (0, 0)
    const3 = lambda i: (0, 0, 0)
    return pl.pallas_call(
        functools.partial(_mixer_body, seq=t),
        grid=(t // bm,),
        in_specs=[
            pl.BlockSpec((bm, d_in), lambda i: (i, 0)),
            pl.BlockSpec((HALO, d_pool), lambda i: (jnp.maximum(i * (bm // HALO) - 1, 0), 0)),
            pl.BlockSpec((HALO, d_pool), lambda i: (jnp.minimum((i + 1) * (bm // HALO), nh - 1), 0)),
            pl.BlockSpec((bm, d), lambda i: (i, 0)),
            pl.BlockSpec(pool_w.shape, const3),
            pl.BlockSpec((1, d_pool), const2),
            pl.BlockSpec((1, d_sgu), const2),
            pl.BlockSpec(sgu_w.shape, const3),
            pl.BlockSpec(sgu_bias.shape, const2),
            pl.BlockSpec((d_mix, d), const2),
        ],
        out_specs=pl.BlockSpec((bm, d), lambda i: (i, 0)),
        out_shape=jax.ShapeDtypeStruct((t, d), F32),
        scratch_shapes=[
            pltpu.VMEM((bm + 2 * HALO, d_pool), F32),
            pltpu.VMEM((bm, d_sgu), BF16),
            pltpu.VMEM((bm, d_mix), BF16),
        ],
        compiler_params=pltpu.CompilerParams(
            dimension_semantics=("arbitrary",),
            vmem_limit_bytes=_vmem_limit(2 * (bm * d_in * 4 + 2 * bm * d * 4 + d_mix * d * 2) + 20 * 2**20)),
        name="mixer",
    )(z, z, z, x, pool_w, pool_scale, norm_g, sgu_w, sgu_bias, w_out)


def _dense_ffn_body(x_ref, g_ref, wg_ref, wu_ref, wd_ref, o_ref, h_ref, *, half):
    @pl.when(pl.program_id(1) == 0)
    def _():
        x = x_ref[...]
        h_ref[...] = _rmsnorm(x, g_ref[...]).astype(BF16)
        o_ref[...] = x

    for j in range(h_ref.shape[0] // half):
        rows = pl.ds(j * half, half)
        h = h_ref[rows, :]
        gate = jnp.dot(h, wg_ref[...], preferred_element_type=F32)
        up = jnp.dot(h, wu_ref[...], preferred_element_type=F32)
        act = (_silu(gate) * up).astype(BF16)
        o_ref[rows, :] += jnp.dot(act, wd_ref[...], preferred_element_type=F32)


def _dense_ffn(x, g, w_gate, w_up, w_down, *, bm, half, tf):
    t, d = x.shape
    d_ff = w_gate.shape[1]
    return pl.pallas_call(
        functools.partial(_dense_ffn_body, half=half),
        grid=(t // bm, d_ff // tf),
        in_specs=[
            pl.BlockSpec((bm, d), lambda i, f: (i, 0)),
            pl.BlockSpec((1, d), lambda i, f: (0, 0)),
            pl.BlockSpec((d, tf), lambda i, f: (0, f)),
            pl.BlockSpec((d, tf), lambda i, f: (0, f)),
            pl.BlockSpec((tf, d), lambda i, f: (f, 0)),
        ],
        out_specs=pl.BlockSpec((bm, d), lambda i, f: (i, 0)),
        out_shape=jax.ShapeDtypeStruct((t, d), F32),
        scratch_shapes=[pltpu.VMEM((bm, d), BF16)],
        compiler_params=pltpu.CompilerParams(
            dimension_semantics=("arbitrary", "arbitrary"),
            vmem_limit_bytes=_vmem_limit(2 * (2 * bm * d * 4 + 3 * d * tf * 2) + bm * d * 2
                                         + 3 * half * tf * 4 + 8 * 2**20)),
        name="dense_ffn",
    )(x, g, w_gate, w_up, w_down)


def _router_body(x_ref, g_ref, rw_ref, hp_ref, ri_ref, rg_ref, cnt_ref, run_ref, *, n_experts):
    i = pl.program_id(0)
    bm, d = x_ref.shape

    @pl.when(i == 0)
    def _():
        run_ref[...] = jnp.zeros_like(run_ref)

    h = _rmsnorm(x_ref[...], g_ref[...])

    bits = pltpu.bitcast(h.astype(BF16).astype(F32), jnp.uint32)
    hp_ref[...] = (bits[:, 0:d // 2] >> 16) | (bits[:, d // 2:d] & jnp.uint32(0xFFFF0000))

    logits = jnp.dot(h, rw_ref[...], preferred_element_type=F32, precision=lax.Precision.HIGHEST)
    lane = lax.broadcasted_iota(jnp.int32, (bm, V7X_LANES), 1)
    neg = jnp.float32(-jnp.inf)
    lg = jnp.where(lane < n_experts, logits, neg)
    m1 = jnp.max(lg, axis=-1, keepdims=True)
    e1 = jnp.min(jnp.where(lg == m1, lane, V7X_LANES), axis=-1, keepdims=True)
    lg2 = jnp.where(lane == e1, neg, lg)
    m2 = jnp.max(lg2, axis=-1, keepdims=True)
    e2 = jnp.min(jnp.where(lg2 == m2, lane, V7X_LANES), axis=-1, keepdims=True)
    ex = jnp.exp(m2 - m1)
    g1 = 1.0 / (1.0 + ex)
    g2 = ex / (1.0 + ex)

    sel = jnp.where((lane == e1) | (lane == e2), 1.0, 0.0)
    r = lax.broadcasted_iota(jnp.int32, (bm, bm), 0)
    c = lax.broadcasted_iota(jnp.int32, (bm, bm), 1)
    tri = jnp.where(c < r, 1.0, 0.0).astype(BF16)
    prefix = jnp.dot(tri, sel.astype(BF16), preferred_element_type=F32) + run_ref[0:1, :]
    rank1 = jnp.sum(jnp.where(lane == e1, prefix, 0.0), axis=-1, keepdims=True).astype(jnp.int32)
    rank2 = jnp.sum(jnp.where(lane == e2, prefix, 0.0), axis=-1, keepdims=True).astype(jnp.int32)
    run_ref[0:1, :] = run_ref[0:1, :] + jnp.sum(sel, axis=0, keepdims=True)

    ri_ref[...] = jnp.where(lane == 0, e1, jnp.where(lane == 1, e2,
                            jnp.where(lane == 2, rank1, jnp.where(lane == 3, rank2, 0))))
    rg_ref[...] = jnp.where(lane == 0, g1, jnp.where(lane == 1, g2, 0.0))
    cnt_ref[...] = run_ref[...].astype(jnp.int32)


def _router(x, g, router_w_padded, *, bm, n_experts):
    t, d = x.shape
    return pl.pallas_call(
        functools.partial(_router_body, n_experts=n_experts),
        grid=(t // bm,),
        in_specs=[
            pl.BlockSpec((bm, d), lambda i: (i, 0)),
            pl.BlockSpec((1, d), lambda i: (0, 0)),
            pl.BlockSpec((d, V7X_LANES), lambda i: (0, 0)),
        ],
        out_specs=[
            pl.BlockSpec((bm, d // 2), lambda i: (i, 0)),
            pl.BlockSpec((bm, V7X_LANES), lambda i: (i, 0)),
            pl.BlockSpec((bm, V7X_LANES), lambda i: (i, 0)),
            pl.BlockSpec((V7X_SUBLANES, V7X_LANES), lambda i: (0, 0)),
        ],
        out_shape=[
            jax.ShapeDtypeStruct((t, d // 2), jnp.uint32),
            jax.ShapeDtypeStruct((t, V7X_LANES), jnp.int32),
            jax.ShapeDtypeStruct((t, V7X_LANES), F32),
            jax.ShapeDtypeStruct((V7X_SUBLANES, V7X_LANES), jnp.int32),
        ],
        scratch_shapes=[pltpu.VMEM((V7X_SUBLANES, V7X_LANES), F32)],
        compiler_params=pltpu.CompilerParams(dimension_semantics=("arbitrary",)),
        name="router",
    )(x, g, router_w_padded)


def _row_copy(src_hbm, dst_ref, sem, src_row, dst_row):
    return pltpu.make_async_copy(src_hbm.at[pl.ds(src_row, 1)], dst_ref.at[pl.ds(dst_row, 1)], sem)


def _expert_body(exp_ref, nv_ref, tok_ref, tokn_ref, hp_hbm, wg_ref, wu_ref, wd_ref, ys_hbm,
                 acc_ref, xw_ref, xs_ref, wgb_ref, wub_ref, wdb_ref, gsem, wsem, *, sub, half, rpf, n_gather):
    s = pl.program_id(0)
    f = pl.program_id(1)
    ns = pl.num_programs(0)
    nf = pl.num_programs(1)
    sb, d = acc_ref.shape
    nsub = sb // sub
    nvalid = nv_ref[s]
    slot = s % 2

    def writeback(blk):
        return pltpu.make_async_copy(acc_ref, ys_hbm.at[pl.ds(blk * sb, sb)], wsem)

    def gather(tokens_ref, buf, r):
        return _row_copy(hp_hbm, xw_ref.at[buf], gsem.at[buf], tokens_ref[jnp.minimum(r, sb - 1)], r)

    def drain(buf):
        def body(r, carry):
            _row_copy(hp_hbm, xw_ref.at[buf], gsem.at[buf], 0, r).wait()
            return carry
        lax.fori_loop(0, n_gather, body, 0, unroll=8)

    @pl.when(f == 0)
    def _():
        @pl.when(s == 0)
        def _():
            def issue(r, carry):
                gather(tok_ref, slot, r).start()
                return carry
            lax.fori_loop(0, n_gather, issue, 0, unroll=8)

        @pl.when(s > 0)
        def _():
            writeback(s - 1).wait()
        acc_ref[...] = jnp.zeros_like(acc_ref)

        drain(slot)
        for j in range(nsub):
            rows = pl.ds(j * sub, sub)

            @pl.when(j * sub < nvalid)
            def _():
                w = xw_ref[slot, rows, :]
                xs_ref[rows, 0:d // 2] = pltpu.bitcast(w << 16, F32).astype(BF16)
                xs_ref[rows, d // 2:d] = pltpu.bitcast(w & jnp.uint32(0xFFFF0000), F32).astype(BF16)

    def cast_weights():
        wgb_ref[...] = wg_ref[...].astype(BF16)
        wub_ref[...] = wu_ref[...].astype(BF16)
        wdb_ref[...] = wd_ref[...].astype(BF16)

    def ffn(rows):
        x = xs_ref[rows, :]
        gate = jnp.dot(x, wgb_ref[...], preferred_element_type=F32)
        up = jnp.dot(x, wub_ref[...], preferred_element_type=F32)
        act = (_silu(gate) * up).astype(BF16)
        acc_ref[rows, :] += jnp.dot(act, wdb_ref[...], preferred_element_type=F32)

    full = nvalid > (nsub - 1) * sub
    n_halves = sb // half
    per_half = rpf // n_halves

    @pl.when(full)
    def _():
        cast_weights()
        for j in range(n_halves):
            for k in range(per_half):
                gather(tokn_ref, 1 - slot, f * rpf + j * per_half + k).start()
            ffn(pl.ds(j * half, half))

    @pl.when(jnp.logical_not(full))
    def _():
        def issue(k, carry):
            gather(tokn_ref, 1 - slot, f * rpf + k).start()
            return carry
        lax.fori_loop(0, rpf, issue, 0, unroll=8)

    @pl.when(jnp.logical_and(nvalid > 0, jnp.logical_not(full)))
    def _():
        cast_weights()
        for j in range(nsub - 1):
            @pl.when(j * sub < nvalid)
            def _():
                ffn(pl.ds(j * sub, sub))

    @pl.when(f == nf - 1)
    def _():
        writeback(s).start()

        @pl.when(s == ns - 1)
        def _():
            writeback(s).wait()
            drain(1 - slot)


def _experts(sb_expert, sb_nvalid, slot_tok, h_packed, w_gate, w_up, w_down, *, layer, sb, sub, half, tf):
    n_sb = sb_expert.shape[0]
    _, _, d, d_ff = w_gate.shape
    nf = d_ff // tf
    n_halves = sb // half
    rpf = pl.cdiv(pl.cdiv(sb, nf), n_halves) * n_halves
    n_gather = rpf * nf

    def f_idx(s, f, nv_ref):
        return jnp.where(nv_ref[s] > 0, f, nf - 1)

    grid_spec = pltpu.PrefetchScalarGridSpec(
        num_scalar_prefetch=2,
        grid=(n_sb, nf),
        in_specs=[
            pl.BlockSpec((sb,), lambda s, f, e, nv: (s,), memory_space=pltpu.SMEM),
            pl.BlockSpec((sb,), lambda s, f, e, nv: (jnp.minimum(s + 1, n_sb - 1),), memory_space=pltpu.SMEM),
            pl.BlockSpec(memory_space=pl.ANY),
            pl.BlockSpec((None, None, d, tf), lambda s, f, e, nv: (layer, e[s], 0, f_idx(s, f, nv))),
            pl.BlockSpec((None, None, d, tf), lambda s, f, e, nv: (layer, e[s], 0, f_idx(s, f, nv))),
            pl.BlockSpec((None, None, tf, d), lambda s, f, e, nv: (layer, e[s], f_idx(s, f, nv), 0)),
        ],
        out_specs=pl.BlockSpec(memory_space=pl.ANY),
        scratch_shapes=[
            pltpu.VMEM((sb, d), F32),
            pltpu.VMEM((2, pl.cdiv(n_gather, V7X_SUBLANES) * V7X_SUBLANES, d // 2), jnp.uint32),
            pltpu.VMEM((sb, d), BF16),
            pltpu.VMEM((d, tf), BF16),
            pltpu.VMEM((d, tf), BF16),
            pltpu.VMEM((tf, d), BF16),
            pltpu.SemaphoreType.DMA((2,)),
            pltpu.SemaphoreType.DMA,
        ],
    )
    weight_bytes = 3 * d * tf * 4
    return pl.pallas_call(
        functools.partial(_expert_body, sub=sub, half=half, rpf=rpf, n_gather=n_gather),
        grid_spec=grid_spec,
        out_shape=jax.ShapeDtypeStruct((n_sb * sb, d), F32),
        compiler_params=pltpu.CompilerParams(
            dimension_semantics=("arbitrary", "arbitrary"),
            vmem_limit_bytes=_vmem_limit(2 * weight_bytes + weight_bytes // 2 + sb * d * (4 + 2)
                                         + 2 * n_gather * d * 2 + 3 * half * tf * 4 + 8 * 2**20)),
        name="experts",
    )(sb_expert, sb_nvalid, slot_tok, slot_tok, h_packed, w_gate, w_up, w_down)


def _combine_body(dest_ref, rg_ref, x_ref, ys_hbm, *rest, final):
    if final:
        fg_ref, o_ref, y1_ref, y2_ref, sem = rest
    else:
        o_ref, y1_ref, y2_ref, sem = rest
    bt = x_ref.shape[0]

    def issue(r, carry):
        _row_copy(ys_hbm, y1_ref, sem, dest_ref[2 * r], r).start()
        _row_copy(ys_hbm, y2_ref, sem, dest_ref[2 * r + 1], r).start()
        return carry
    lax.fori_loop(0, bt, issue, 0, unroll=8)

    def drain(r, carry):
        _row_copy(ys_hbm, y1_ref, sem, 0, r).wait()
        _row_copy(ys_hbm, y2_ref, sem, 0, r).wait()
        return carry
    lax.fori_loop(0, bt, drain, 0, unroll=8)

    out = x_ref[...] + (rg_ref[:, 0:1] * y1_ref[...] + rg_ref[:, 1:2] * y2_ref[...])
    if final:
        out = _rmsnorm(out, fg_ref[...])
    o_ref[...] = out


def _combine(dest, rg, x, ys, final_g, *, bt):
    t, d = x.shape
    final = final_g is not None
    in_specs = [
        pl.BlockSpec((TOP_K * bt,), lambda i: (i,), memory_space=pltpu.SMEM),
        pl.BlockSpec((bt, V7X_LANES), lambda i: (i, 0)),
        pl.BlockSpec((bt, d), lambda i: (i, 0)),
        pl.BlockSpec(memory_space=pl.ANY),
    ]
    args = [dest, rg, x, ys]
    if final:
        in_specs.append(pl.BlockSpec((1, d), lambda i: (0, 0)))
        args.append(final_g)
    return pl.pallas_call(
        functools.partial(_combine_body, final=final),
        grid=(t // bt,),
        in_specs=in_specs,
        out_specs=pl.BlockSpec((bt, d), lambda i: (i, 0)),
        out_shape=jax.ShapeDtypeStruct((t, d), F32),
        scratch_shapes=[pltpu.VMEM((bt, d), F32), pltpu.VMEM((bt, d), F32), pltpu.SemaphoreType.DMA],
        compiler_params=pltpu.CompilerParams(dimension_semantics=("arbitrary",)),
        name="combine",
    )(*args)


def _routing_tables(ri, cnt, *, n_experts, sb, n_sb):
    t = ri.shape[0]
    experts = ri[:, 0:TOP_K]
    ranks = ri[:, TOP_K:2 * TOP_K]
    counts = cnt[0, 0:n_experts]
    nsb_e = (counts + sb - 1) // sb
    sb_end = jnp.cumsum(nsb_e)
    sb_start = sb_end - nsb_e
    dest = (sb_start * sb)[experts] + ranks
    tok = jnp.broadcast_to(jnp.arange(t, dtype=jnp.int32)[:, None], (t, TOP_K))
    slot_tok = jnp.zeros((n_sb * sb,), jnp.int32).at[dest.reshape(-1)].set(tok.reshape(-1))
    ids = jnp.arange(n_sb, dtype=jnp.int32)
    used = ids < sb_end[-1]
    owner_of = jnp.where(used, ids, jnp.maximum(sb_end[-1] - 1, 0))
    sb_expert = jnp.minimum(jnp.sum(sb_end[None, :] <= owner_of[:, None], axis=1), n_experts - 1).astype(jnp.int32)
    nvalid = jnp.clip(counts[sb_expert] - (ids - sb_start[sb_expert]) * sb, 0, sb)
    sb_nvalid = jnp.where(used, nvalid, 0).astype(jnp.int32)
    return sb_expert, sb_nvalid, slot_tok, dest.reshape(-1).astype(jnp.int32)


def _block(n, pref, mult):
    if n <= pref:
        return n
    b = (pref // mult) * mult
    while n % b:
        b -= mult
    return b


def _forward(x, norm_mix_g, w_in, pool_w, pool_scale, sgu_norm_g, sgu_w, sgu_b, w_out, norm_ffn_g,
             ffn_w_gate, ffn_w_up, ffn_w_down, router_w, moe_w_gate, moe_w_up, moe_w_down,
             final_norm_g, *, bm_mix=512, bm_ffn=512, tf=512, sb=1024, sub=256, half=512, bt=256):
    b, s, d = x.shape
    assert b == 1, "pooling windows and gating chunks are laid out for a single sequence"
    t = b * s
    depth = w_in.shape[0]
    n_experts = router_w.shape[-1]
    n_heads, chunk = sgu_b.shape[1:]
    d_sgu = sgu_norm_g.shape[1]
    assert chunk == SGU_CHUNK and depth % 2 == 0 and n_experts <= V7X_LANES
    bm_mix = _block(t, bm_mix, SGU_CHUNK)
    bm_ffn = _block(t, bm_ffn, V7X_SUBLANES)
    bt = _block(t, bt, V7X_SUBLANES)
    sb = min(sb, t)
    sub = min(sub, sb)
    half = min(half, sb)
    tf = _block(ffn_w_gate.shape[-1], tf, V7X_LANES)
    n_sb = (t * TOP_K + n_experts * (sb - 1)) // sb

    x = x.reshape(t, d)
    row = lambda a: a.reshape(1, -1)
    for l in range(depth):
        z = _in_proj(x, row(norm_mix_g[l]), w_in[l].astype(BF16), bm=bm_mix)
        bias = jnp.broadcast_to(sgu_b[l].T[:, :, None], (chunk, n_heads, d_sgu // n_heads)).reshape(chunk, d_sgu)
        x = _mixer(z, x, pool_w[l].astype(BF16), row(pool_scale[l]), row(sgu_norm_g[l]),
                   sgu_w[l].astype(BF16), bias, w_out[l].astype(BF16), bm=bm_mix)
        i = l // 2
        if l % 2 == 0:
            x = _dense_ffn(x, row(norm_ffn_g[l]), ffn_w_gate[i].astype(BF16), ffn_w_up[i].astype(BF16),
                           ffn_w_down[i].astype(BF16), bm=_block(t, sb, half), half=half, tf=tf)
        else:
            rw = jnp.pad(router_w[i], ((0, 0), (0, V7X_LANES - n_experts)))
            h_packed, ri, rg, cnt = _router(x, row(norm_ffn_g[l]), rw, bm=bm_ffn, n_experts=n_experts)
            sb_expert, sb_nvalid, slot_tok, dest = _routing_tables(ri, cnt, n_experts=n_experts, sb=sb, n_sb=n_sb)
            ys = _experts(sb_expert, sb_nvalid, slot_tok, h_packed, moe_w_gate, moe_w_up, moe_w_down,
                          layer=i, sb=sb, sub=sub, half=half, tf=tf)
            x = _combine(dest, rg, x, ys, row(final_norm_g) if l == depth - 1 else None, bt=bt)
    return x.reshape(b, s, d)


def kernel(x, norm_mix_g, w_in, pool_w, pool_scale, sgu_norm_g, sgu_w, sgu_b, w_out, norm_ffn_g,
           ffn_w_gate, ffn_w_up, ffn_w_down, router_w, moe_w_gate, moe_w_up, moe_w_down, final_norm_g):
    return _forward(x, norm_mix_g, w_in, pool_w, pool_scale, sgu_norm_g, sgu_w, sgu_b, w_out, norm_ffn_g,
                    ffn_w_gate, ffn_w_up, ffn_w_down, router_w, moe_w_gate, moe_w_up, moe_w_down,
                    final_norm_g)
```

```python
import functools

import jax
import jax.numpy as jnp
from jax import lax
from jax.experimental import pallas as pl
from jax.experimental.pallas import tpu as pltpu

RMS_EPS = 1e-6
LN_EPS = 1e-5
POOL_WINDOWS = (2, 4, 8, 16)
SGU_CHUNK = 128
TOP_K = 2

V7X_LANES = 128
V7X_SUBLANES = 8
V7X_VMEM_BYTES = 64 * 1024 * 1024
HALO = max(POOL_WINDOWS) // 2
assert HALO == V7X_SUBLANES

BF16 = jnp.bfloat16
F32 = jnp.float32


def _vmem_limit(nbytes):
    return int(min(nbytes, V7X_VMEM_BYTES - 6 * 1024 * 1024))


def _rmsnorm(x, g):
    return x * lax.rsqrt(jnp.mean(x * x, axis=-1, keepdims=True) + RMS_EPS) * g


def _gelu(x):
    return 0.5 * x * (1.0 + lax.erf(x * (2.0 ** -0.5)))


def _silu(x):
    return x / (1.0 + jnp.exp(-x))


def _in_proj_body(x_ref, g_ref, w_ref, z_ref):
    h = _rmsnorm(x_ref[...], g_ref[...]).astype(BF16)
    z_ref[...] = jnp.dot(h, w_ref[...], preferred_element_type=F32)


def _in_proj(x, g, w, *, bm):
    t, d = x.shape
    d_in = w.shape[1]
    return pl.pallas_call(
        _in_proj_body,
        grid=(t // bm,),
        in_specs=[
            pl.BlockSpec((bm, d), lambda i: (i, 0)),
            pl.BlockSpec((1, d), lambda i: (0, 0)),
            pl.BlockSpec((d, d_in), lambda i: (0, 0)),
        ],
        out_specs=pl.BlockSpec((bm, d_in), lambda i: (i, 0)),
        out_shape=jax.ShapeDtypeStruct((t, d_in), F32),
        compiler_params=pltpu.CompilerParams(
            dimension_semantics=("arbitrary",),
            vmem_limit_bytes=_vmem_limit(2 * (bm * d * 4 + d * d_in * 2 + bm * d_in * 4) + 16 * 2**20)),
        name="in_proj",
    )(x, g, w)


def _mixer_body(z_ref, zp_ref, zn_ref, x_ref, pw_ref, ps_ref, ng_ref, sw_ref, sb_ref, wo_ref,
                o_ref, ext_ref, vn_ref, y_ref, *, seq):
    i = pl.program_id(0)
    bm = z_ref.shape[0]
    d_pool = ps_ref.shape[1]
    d_sgu = ng_ref.shape[1]
    pg = d_pool // len(POOL_WINDOWS)
    n_heads = sw_ref.shape[0]
    hd = d_sgu // n_heads

    ext_ref[0:HALO, :] = jnp.where(i > 0, zp_ref[...], 0.0)
    ext_ref[HALO:HALO + bm, :] = z_ref[:, 0:d_pool]
    ext_ref[HALO + bm:2 * HALO + bm, :] = jnp.where(i < pl.num_programs(0) - 1, zn_ref[...], 0.0)
    t = i * bm + lax.broadcasted_iota(jnp.int32, (bm, 1), 0)
    for g, w in enumerate(POOL_WINDOWS):
        cs = slice(g * pg, (g + 1) * pg)
        first = HALO - w // 2
        acc = ext_ref[first:first + bm, cs]
        for j in range(1, w):
            acc = acc + ext_ref[first + j:first + j + bm, cs]
        cnt = jnp.minimum(t + (w - w // 2), seq) - jnp.maximum(t - w // 2, 0)
        diff = acc * (1.0 / cnt.astype(F32)) - z_ref[:, cs]
        yp = jnp.dot(diff.astype(BF16), pw_ref[g], preferred_element_type=F32) * ps_ref[:, cs]
        y_ref[:, cs] = yp.astype(BF16)

    v = _gelu(z_ref[:, d_pool + d_sgu:d_pool + 2 * d_sgu])
    mu = jnp.mean(v, axis=-1, keepdims=True)
    vc = v - mu
    var = jnp.mean(vc * vc, axis=-1, keepdims=True)
    vn_ref[...] = (vc * lax.rsqrt(var + LN_EPS) * ng_ref[...]).astype(BF16)
    for c in range(bm // SGU_CHUNK):
        rs = slice(c * SGU_CHUNK, (c + 1) * SGU_CHUNK)
        for h in range(n_heads):
            hs = slice(h * hd, (h + 1) * hd)
            mixed = jnp.dot(sw_ref[h], vn_ref[rs, hs], preferred_element_type=F32) + sb_ref[:, hs]
            u = _gelu(z_ref[rs, d_pool + h * hd:d_pool + (h + 1) * hd])
            y_ref[rs, d_pool + h * hd:d_pool + (h + 1) * hd] = (u * mixed).astype(BF16)

    o_ref[...] = x_ref[...] + jnp.dot(y_ref[...], wo_ref[...], preferred_element_type=F32)


def _mixer(z, x, pool_w, pool_scale, norm_g, sgu_w, sgu_bias, w_out, *, bm):
    t, d = x.shape
    d_in = z.shape[1]
    d_pool = pool_scale.shape[1]
    d_sgu = norm_g.shape[1]
    d_mix = d_pool + d_sgu
    nh = t // HALO
    const2 = lambda i: (0, 0)
    const3 = lambda i: (0, 0, 0)
    return pl.pallas_call(
        functools.partial(_mixer_body, seq=t),
        grid=(t // bm,),
        in_specs=[
            pl.BlockSpec((bm, d_in), lambda i: (i, 0)),
            pl.BlockSpec((HALO, d_pool), lambda i: (jnp.maximum(i * (bm // HALO) - 1, 0), 0)),
            pl.BlockSpec((HALO, d_pool), lambda i: (jnp.minimum((i + 1) * (bm // HALO), nh - 1), 0)),
            pl.BlockSpec((bm, d), lambda i: (i, 0)),
            pl.BlockSpec(pool_w.shape, const3),
            pl.BlockSpec((1, d_pool), const2),
            pl.BlockSpec((1, d_sgu), const2),
            pl.BlockSpec(sgu_w.shape, const3),
            pl.BlockSpec(sgu_bias.shape, const2),
            pl.BlockSpec((d_mix, d), const2),
        ],
        out_specs=pl.BlockSpec((bm, d), lambda i: (i, 0)),
        out_shape=jax.ShapeDtypeStruct((t, d), F32),
        scratch_shapes=[
            pltpu.VMEM((bm + 2 * HALO, d_pool), F32),
            pltpu.VMEM((bm, d_sgu), BF16),
            pltpu.VMEM((bm, d_mix), BF16),
        ],
        compiler_params=pltpu.CompilerParams(
            dimension_semantics=("arbitrary",),
            vmem_limit_bytes=_vmem_limit(2 * (bm * d_in * 4 + 2 * bm * d * 4 + d_mix * d * 2) + 20 * 2**20)),
        name="mixer",
    )(z, z, z, x, pool_w, pool_scale, norm_g, sgu_w, sgu_bias, w_out)


def _dense_ffn_body(x_ref, g_ref, wg_ref, wu_ref, wd_ref, o_ref, h_ref, *, half):
    @pl.when(pl.program_id(1) == 0)
    def _():
        x = x_ref[...]
        h_ref[...] = _rmsnorm(x, g_ref[...]).astype(BF16)
        o_ref[...] = x

    for j in range(h_ref.shape[0] // half):
        rows = pl.ds(j * half, half)
        h = h_ref[rows, :]
        gate = jnp.dot(h, wg_ref[...], preferred_element_type=F32)
        up = jnp.dot(h, wu_ref[...], preferred_element_type=F32)
        act = (_silu(gate) * up).astype(BF16)
        o_ref[rows, :] += jnp.dot(act, wd_ref[...], preferred_element_type=F32)


def _dense_ffn(x, g, w_gate, w_up, w_down, *, bm, half, tf):
    t, d = x.shape
    d_ff = w_gate.shape[1]
    return pl.pallas_call(
        functools.partial(_dense_ffn_body, half=half),
        grid=(t // bm, d_ff // tf),
        in_specs=[
            pl.BlockSpec((bm, d), lambda i, f: (i, 0)),
            pl.BlockSpec((1, d), lambda i, f: (0, 0)),
            pl.BlockSpec((d, tf), lambda i, f: (0, f)),
            pl.BlockSpec((d, tf), lambda i, f: (0, f)),
            pl.BlockSpec((tf, d), lambda i, f: (f, 0)),
        ],
        out_specs=pl.BlockSpec((bm, d), lambda i, f: (i, 0)),
        out_shape=jax.ShapeDtypeStruct((t, d), F32),
        scratch_shapes=[pltpu.VMEM((bm, d), BF16)],
        compiler_params=pltpu.CompilerParams(
            dimension_semantics=("arbitrary", "arbitrary"),
            vmem_limit_bytes=_vmem_limit(2 * (2 * bm * d * 4 + 3 * d * tf * 2) + bm * d * 2
                                         + 3 * half * tf * 4 + 8 * 2**20)),
        name="dense_ffn",
    )(x, g, w_gate, w_up, w_down)


def _router_body(x_ref, g_ref, rw_ref, hp_ref, ri_ref, rg_ref, cnt_ref, run_ref, *, n_experts):
    i = pl.program_id(0)
    bm, d = x_ref.shape

    @pl.when(i == 0)
    def _():
        run_ref[...] = jnp.zeros_like(run_ref)

    h = _rmsnorm(x_ref[...], g_ref[...])

    h_hi = h.astype(BF16)
    bits = pltpu.bitcast(h_hi.astype(F32), jnp.uint32)
    hp_ref[...] = (bits[:, 0:d // 2] >> 16) | (bits[:, d // 2:d] & jnp.uint32(0xFFFF0000))

    h_lo = (h - h_hi.astype(F32)).astype(BF16)
    rw = rw_ref[...]
    rw_hi = rw.astype(BF16)
    rw_lo = (rw - rw_hi.astype(F32)).astype(BF16)
    logits = (jnp.dot(h_hi, rw_hi, preferred_element_type=F32)
              + (jnp.dot(h_lo, rw_hi, preferred_element_type=F32) + jnp.dot(h_hi, rw_lo, preferred_element_type=F32)))
    lane = lax.broadcasted_iota(jnp.int32, (bm, V7X_LANES), 1)
    neg = jnp.float32(-jnp.inf)
    lg = jnp.where(lane < n_experts, logits, neg)
    m1 = jnp.max(lg, axis=-1, keepdims=True)
    e1 = jnp.min(jnp.where(lg == m1, lane, V7X_LANES), axis=-1, keepdims=True)
    lg2 = jnp.where(lane == e1, neg, lg)
    m2 = jnp.max(lg2, axis=-1, keepdims=True)
    e2 = jnp.min(jnp.where(lg2 == m2, lane, V7X_LANES), axis=-1, keepdims=True)
    ex = jnp.exp(m2 - m1)
    g1 = 1.0 / (1.0 + ex)
    g2 = ex / (1.0 + ex)

    sel = jnp.where((lane == e1) | (lane == e2), 1.0, 0.0)
    r = lax.broadcasted_iota(jnp.int32, (bm, bm), 0)
    c = lax.broadcasted_iota(jnp.int32, (bm, bm), 1)
    tri = jnp.where(c < r, 1.0, 0.0).astype(BF16)
    prefix = jnp.dot(tri, sel.astype(BF16), preferred_element_type=F32) + run_ref[0:1, :]
    rank1 = jnp.sum(jnp.where(lane == e1, prefix, 0.0), axis=-1, keepdims=True).astype(jnp.int32)
    rank2 = jnp.sum(jnp.where(lane == e2, prefix, 0.0), axis=-1, keepdims=True).astype(jnp.int32)
    run_ref[0:1, :] = run_ref[0:1, :] + jnp.sum(sel, axis=0, keepdims=True)

    ri_ref[...] = jnp.where(lane == 0, e1, jnp.where(lane == 1, e2,
                            jnp.where(lane == 2, rank1, jnp.where(lane == 3, rank2, 0))))
    rg_ref[...] = jnp.where(lane == 0, g1, jnp.where(lane == 1, g2, 0.0))
    cnt_ref[...] = run_ref[...].astype(jnp.int32)


def _router(x, g, router_w_padded, *, bm, n_experts):
    t, d = x.shape
    return pl.pallas_call(
        functools.partial(_router_body, n_experts=n_experts),
        grid=(t // bm,),
        in_specs=[
            pl.BlockSpec((bm, d), lambda i: (i, 0)),
            pl.BlockSpec((1, d), lambda i: (0, 0)),
            pl.BlockSpec((d, V7X_LANES), lambda i: (0, 0)),
        ],
        out_specs=[
            pl.BlockSpec((bm, d // 2), lambda i: (i, 0)),
            pl.BlockSpec((bm, V7X_LANES), lambda i: (i, 0)),
            pl.BlockSpec((bm, V7X_LANES), lambda i: (i, 0)),
            pl.BlockSpec((V7X_SUBLANES, V7X_LANES), lambda i: (0, 0)),
        ],
        out_shape=[
            jax.ShapeDtypeStruct((t, d // 2), jnp.uint32),
            jax.ShapeDtypeStruct((t, V7X_LANES), jnp.int32),
            jax.ShapeDtypeStruct((t, V7X_LANES), F32),
            jax.ShapeDtypeStruct((V7X_SUBLANES, V7X_LANES), jnp.int32),
        ],
        scratch_shapes=[pltpu.VMEM((V7X_SUBLANES, V7X_LANES), F32)],
        compiler_params=pltpu.CompilerParams(dimension_semantics=("arbitrary",)),
        name="router",
    )(x, g, router_w_padded)


def _row_copy(src_hbm, dst_ref, sem, src_row, dst_row):
    return pltpu.make_async_copy(src_hbm.at[pl.ds(src_row, 1)], dst_ref.at[pl.ds(dst_row, 1)], sem)


def _expert_body(exp_ref, nv_ref, tok_ref, tokn_ref, hp_hbm, wg_ref, wu_ref, wd_ref, ys_hbm,
                 acc_ref, xw_ref, xs_ref, wgb_ref, wub_ref, wdb_ref, gsem, wsem, *, sub, half, rpf, n_gather):
    s = pl.program_id(0)
    f = pl.program_id(1)
    ns = pl.num_programs(0)
    nf = pl.num_programs(1)
    sb, d = acc_ref.shape
    nsub = sb // sub
    nvalid = nv_ref[s]
    slot = s % 2

    def writeback(blk):
        return pltpu.make_async_copy(acc_ref, ys_hbm.at[pl.ds(blk * sb, sb)], wsem)

    def gather(tokens_ref, buf, r):
        return _row_copy(hp_hbm, xw_ref.at[buf], gsem.at[buf], tokens_ref[jnp.minimum(r, sb - 1)], r)

    def drain(buf):
        def body(r, carry):
            _row_copy(hp_hbm, xw_ref.at[buf], gsem.at[buf], 0, r).wait()
            return carry
        lax.fori_loop(0, n_gather, body, 0, unroll=8)

    @pl.when(f == 0)
    def _():
        @pl.when(s == 0)
        def _():
            def issue(r, carry):
                gather(tok_ref, slot, r).start()
                return carry
            lax.fori_loop(0, n_gather, issue, 0, unroll=8)

        @pl.when(s > 0)
        def _():
            writeback(s - 1).wait()
        acc_ref[...] = jnp.zeros_like(acc_ref)

        drain(slot)
        for j in range(nsub):
            rows = pl.ds(j * sub, sub)

            @pl.when(j * sub < nvalid)
            def _():
                w = xw_ref[slot, rows, :]
                xs_ref[rows, 0:d // 2] = pltpu.bitcast(w << 16, F32).astype(BF16)
                xs_ref[rows, d // 2:d] = pltpu.bitcast(w & jnp.uint32(0xFFFF0000), F32).astype(BF16)

    def cast_weights():
        wgb_ref[...] = wg_ref[...].astype(BF16)
        wub_ref[...] = wu_ref[...].astype(BF16)
        wdb_ref[...] = wd_ref[...].astype(BF16)

    def ffn(rows):
        x = xs_ref[rows, :]
        gate = jnp.dot(x, wgb_ref[...], preferred_element_type=F32)
        up = jnp.dot(x, wub_ref[...], preferred_element_type=F32)
        act = (_silu(gate) * up).astype(BF16)
        acc_ref[rows, :] += jnp.dot(act, wdb_ref[...], preferred_element_type=F32)

    full = nvalid > (nsub - 1) * sub
    n_halves = sb // half

    @pl.when(full)
    def _():
        cast_weights()
        for j in range(n_halves):
            if j == n_halves - 1:
                for k in range(rpf):
                    gather(tokn_ref, 1 - slot, f * rpf + k).start(priority=1)
            ffn(pl.ds(j * half, half))

    @pl.when(jnp.logical_not(full))
    def _():
        def issue(k, carry):
            gather(tokn_ref, 1 - slot, f * rpf + k).start(priority=1)
            return carry
        lax.fori_loop(0, rpf, issue, 0, unroll=8)

    @pl.when(jnp.logical_and(nvalid > 0, jnp.logical_not(full)))
    def _():
        cast_weights()
        for j in range(nsub - 1):
            @pl.when(j * sub < nvalid)
            def _():
                ffn(pl.ds(j * sub, sub))

    @pl.when(f == nf - 1)
    def _():
        writeback(s).start()

        @pl.when(s == ns - 1)
        def _():
            writeback(s).wait()
            drain(1 - slot)


def _experts(sb_expert, sb_nvalid, slot_tok, h_packed, w_gate, w_up, w_down, *, layer, sb, sub, half, tf):
    n_sb = sb_expert.shape[0]
    _, _, d, d_ff = w_gate.shape
    nf = d_ff // tf
    rpf = pl.cdiv(sb, nf)
    n_gather = rpf * nf

    def f_idx(s, f, nv_ref):
        return jnp.where(nv_ref[s] > 0, f, nf - 1)

    grid_spec = pltpu.PrefetchScalarGridSpec(
        num_scalar_prefetch=2,
        grid=(n_sb, nf),
        in_specs=[
            pl.BlockSpec((sb,), lambda s, f, e, nv: (s,), memory_space=pltpu.SMEM),
            pl.BlockSpec((sb,), lambda s, f, e, nv: (jnp.minimum(s + 1, n_sb - 1),), memory_space=pltpu.SMEM),
            pl.BlockSpec(memory_space=pl.ANY),
            pl.BlockSpec((None, None, d, tf), lambda s, f, e, nv: (layer, e[s], 0, f_idx(s, f, nv))),
            pl.BlockSpec((None, None, d, tf), lambda s, f, e, nv: (layer, e[s], 0, f_idx(s, f, nv))),
            pl.BlockSpec((None, None, tf, d), lambda s, f, e, nv: (layer, e[s], f_idx(s, f, nv), 0)),
        ],
        out_specs=pl.BlockSpec(memory_space=pl.ANY),
        scratch_shapes=[
            pltpu.VMEM((sb, d), F32),
            pltpu.VMEM((2, pl.cdiv(n_gather, V7X_SUBLANES) * V7X_SUBLANES, d // 2), jnp.uint32),
            pltpu.VMEM((sb, d), BF16),
            pltpu.VMEM((d, tf), BF16),
            pltpu.VMEM((d, tf), BF16),
            pltpu.VMEM((tf, d), BF16),
            pltpu.SemaphoreType.DMA((2,)),
            pltpu.SemaphoreType.DMA,
        ],
    )
    weight_bytes = 3 * d * tf * 4
    return pl.pallas_call(
        functools.partial(_expert_body, sub=sub, half=half, rpf=rpf, n_gather=n_gather),
        grid_spec=grid_spec,
        out_shape=jax.ShapeDtypeStruct((n_sb * sb, d), F32),
        compiler_params=pltpu.CompilerParams(
            dimension_semantics=("arbitrary", "arbitrary"),
            vmem_limit_bytes=_vmem_limit(2 * weight_bytes + weight_bytes // 2 + sb * d * (4 + 2)
                                         + 2 * n_gather * d * 2 + 3 * half * tf * 4 + 8 * 2**20)),
        name="experts",
    )(sb_expert, sb_nvalid, slot_tok, slot_tok, h_packed, w_gate, w_up, w_down)


def _combine_body(dest_ref, destn_ref, rg_ref, x_ref, ys_hbm, g_ref, *rest, proj):
    if proj:
        w_ref, xo_ref, z_ref, y_ref, sem = rest
    else:
        o_ref, y_ref, sem = rest
    i = pl.program_id(0)
    bt = x_ref.shape[0]
    slot = i % 2

    def fetch(d_ref, buf, r, k):
        return _row_copy(ys_hbm, y_ref.at[buf, k], sem.at[buf], d_ref[TOP_K * r + k], r)

    def drain(buf):
        def body(r, carry):
            for k in range(TOP_K):
                _row_copy(ys_hbm, y_ref.at[buf, k], sem.at[buf], 0, r).wait()
            return carry
        lax.fori_loop(0, bt, body, 0, unroll=8)

    @pl.when(i == 0)
    def _():
        def issue(r, carry):
            for k in range(TOP_K):
                fetch(dest_ref, slot, r, k).start()
            return carry
        lax.fori_loop(0, bt, issue, 0, unroll=8)

    drain(slot)
    out = x_ref[...] + (rg_ref[:, 0:1] * y_ref[slot, 0] + rg_ref[:, 1:2] * y_ref[slot, 1])
    if proj:
        xo_ref[...] = out
    for r in range(bt):
        for k in range(TOP_K):
            fetch(destn_ref, 1 - slot, r, k).start(priority=1)
    h = _rmsnorm(out, g_ref[...])
    if proj:
        z_ref[...] = jnp.dot(h.astype(BF16), w_ref[...], preferred_element_type=F32)
    else:
        o_ref[...] = h

    @pl.when(i == pl.num_programs(0) - 1)
    def _():
        drain(1 - slot)


def _combine(dest, rg, x, ys, g, w=None, *, bt):
    t, d = x.shape
    n = t // bt
    proj = w is not None
    in_specs = [
        pl.BlockSpec((TOP_K * bt,), lambda i: (i,), memory_space=pltpu.SMEM),
        pl.BlockSpec((TOP_K * bt,), lambda i: (jnp.minimum(i + 1, n - 1),), memory_space=pltpu.SMEM),
        pl.BlockSpec((bt, V7X_LANES), lambda i: (i, 0)),
        pl.BlockSpec((bt, d), lambda i: (i, 0)),
        pl.BlockSpec(memory_space=pl.ANY),
        pl.BlockSpec((1, d), lambda i: (0, 0)),
    ]
    args = [dest, dest, rg, x, ys, g]
    out_specs = [pl.BlockSpec((bt, d), lambda i: (i, 0))]
    out_shape = [jax.ShapeDtypeStruct((t, d), F32)]
    nbytes = 2 * 2 * bt * d * 4 + 2 * TOP_K * bt * d * 4
    if proj:
        d_in = w.shape[1]
        in_specs.append(pl.BlockSpec((d, d_in), lambda i: (0, 0)))
        args.append(w)
        out_specs.append(pl.BlockSpec((bt, d_in), lambda i: (i, 0)))
        out_shape.append(jax.ShapeDtypeStruct((t, d_in), F32))
        nbytes += 2 * (bt * d_in * 4 + d * d_in * 2)
    res = pl.pallas_call(
        functools.partial(_combine_body, proj=proj),
        grid=(n,),
        in_specs=in_specs,
        out_specs=out_specs,
        out_shape=out_shape,
        scratch_shapes=[pltpu.VMEM((2, TOP_K, bt, d), F32), pltpu.SemaphoreType.DMA((2,))],
        compiler_params=pltpu.CompilerParams(
            dimension_semantics=("arbitrary",), vmem_limit_bytes=_vmem_limit(nbytes + 8 * 2**20)),
        name="combine",
    )(*args)
    return tuple(res) if proj else res[0]


def _routing_tables(ri, cnt, *, n_experts, sb, n_sb):
    t = ri.shape[0]
    experts = ri[:, 0:TOP_K]
    ranks = ri[:, TOP_K:2 * TOP_K]
    counts = cnt[0, 0:n_experts]
    nsb_e = (counts + sb - 1) // sb
    sb_end = jnp.cumsum(nsb_e)
    sb_start = sb_end - nsb_e
    dest = (sb_start * sb)[experts] + ranks
    tok = jnp.broadcast_to(jnp.arange(t, dtype=jnp.int32)[:, None], (t, TOP_K))
    slot_tok = jnp.zeros((n_sb * sb,), jnp.int32).at[dest.reshape(-1)].set(tok.reshape(-1))
    ids = jnp.arange(n_sb, dtype=jnp.int32)
    used = ids < sb_end[-1]
    owner_of = jnp.where(used, ids, jnp.maximum(sb_end[-1] - 1, 0))
    sb_expert = jnp.minimum(jnp.sum(sb_end[None, :] <= owner_of[:, None], axis=1), n_experts - 1).astype(jnp.int32)
    nvalid = jnp.clip(counts[sb_expert] - (ids - sb_start[sb_expert]) * sb, 0, sb)
    sb_nvalid = jnp.where(used, nvalid, 0).astype(jnp.int32)
    return sb_expert, sb_nvalid, slot_tok, dest.reshape(-1).astype(jnp.int32)


def _block(n, pref, mult):
    if n <= pref:
        return n
    b = (pref // mult) * mult
    while n % b:
        b -= mult
    return b


def _forward(x, norm_mix_g, w_in, pool_w, pool_scale, sgu_norm_g, sgu_w, sgu_b, w_out, norm_ffn_g,
             ffn_w_gate, ffn_w_up, ffn_w_down, router_w, moe_w_gate, moe_w_up, moe_w_down,
             final_norm_g, *, bm_mix=512, bm_ffn=512, tf=512, sb=1024, sub=256, half=512, bt=256):
    b, s, d = x.shape
    assert b == 1, "pooling windows and gating chunks are laid out for a single sequence"
    t = b * s
    depth = w_in.shape[0]
    n_experts = router_w.shape[-1]
    n_heads, chunk = sgu_b.shape[1:]
    d_sgu = sgu_norm_g.shape[1]
    assert chunk == SGU_CHUNK and depth % 2 == 0 and n_experts <= V7X_LANES
    bm_mix = _block(t, bm_mix, SGU_CHUNK)
    bm_ffn = _block(t, bm_ffn, V7X_SUBLANES)
    bt = _block(t, bt, V7X_SUBLANES)
    sb = min(sb, t)
    sub = min(sub, sb)
    half = min(half, sb)
    tf = _block(ffn_w_gate.shape[-1], tf, V7X_LANES)
    n_sb = (t * TOP_K + n_experts * (sb - 1)) // sb

    x = x.reshape(t, d)
    row = lambda a: a.reshape(1, -1)
    z = None
    for l in range(depth):
        if z is None:
            z = _in_proj(x, row(norm_mix_g[l]), w_in[l].astype(BF16), bm=bm_mix)
        bias = jnp.broadcast_to(sgu_b[l].T[:, :, None], (chunk, n_heads, d_sgu // n_heads)).reshape(chunk, d_sgu)
        x = _mixer(z, x, pool_w[l].astype(BF16), row(pool_scale[l]), row(sgu_norm_g[l]),
                   sgu_w[l].astype(BF16), bias, w_out[l].astype(BF16), bm=bm_mix)
        z = None
        i = l // 2
        if l % 2 == 0:
            x = _dense_ffn(x, row(norm_ffn_g[l]), ffn_w_gate[i].astype(BF16), ffn_w_up[i].astype(BF16),
                           ffn_w_down[i].astype(BF16), bm=_block(t, sb, half), half=half, tf=tf)
        else:
            rw = jnp.pad(router_w[i], ((0, 0), (0, V7X_LANES - n_experts)))
            h_packed, ri, rg, cnt = _router(x, row(norm_ffn_g[l]), rw, bm=bm_ffn, n_experts=n_experts)
            sb_expert, sb_nvalid, slot_tok, dest = _routing_tables(ri, cnt, n_experts=n_experts, sb=sb, n_sb=n_sb)
            ys = _experts(sb_expert, sb_nvalid, slot_tok, h_packed, moe_w_gate, moe_w_up, moe_w_down,
                          layer=i, sb=sb, sub=sub, half=half, tf=tf)
            if l == depth - 1:
                x = _combine(dest, rg, x, ys, row(final_norm_g), bt=bt)
            else:
                x, z = _combine(dest, rg, x, ys, row(norm_mix_g[l + 1]), w_in[l + 1].astype(BF16), bt=bt)
    return x.reshape(b, s, d)


def kernel(x, norm_mix_g, w_in, pool_w, pool_scale, sgu_norm_g, sgu_w, sgu_b, w_out, norm_ffn_g,
           ffn_w_gate, ffn_w_up, ffn_w_down, router_w, moe_w_gate, moe_w_up, moe_w_down, final_norm_g):
    return _forward(x, norm_mix_g, w_in, pool_w, pool_scale, sgu_norm_g, sgu_w, sgu_b, w_out, norm_ffn_g,
                    ffn_w_gate, ffn_w_up, ffn_w_down, router_w, moe_w_gate, moe_w_up, moe_w_down,
                    final_norm_g)
```

```python
import functools

import jax
import jax.numpy as jnp
from jax import lax
from jax.experimental import pallas as pl
from jax.experimental.pallas import tpu as pltpu

RMS_EPS = 1e-6
LN_EPS = 1e-5
POOL_WINDOWS = (2, 4, 8, 16)
SGU_CHUNK = 128
TOP_K = 2

V7X_LANES = 128
V7X_SUBLANES = 8
V7X_VMEM_BYTES = 64 * 1024 * 1024
HALO = max(POOL_WINDOWS) // 2
assert HALO == V7X_SUBLANES

BF16 = jnp.bfloat16
F32 = jnp.float32


def _vmem_limit(nbytes):
    return int(min(nbytes, V7X_VMEM_BYTES - 6 * 1024 * 1024))


def _rmsnorm(x, g):
    return x * lax.rsqrt(jnp.mean(x * x, axis=-1, keepdims=True) + RMS_EPS) * g


def _gelu(x):
    return 0.5 * x * (1.0 + lax.erf(x * (2.0 ** -0.5)))


def _silu(x):
    return x / (1.0 + jnp.exp(-x))


def _in_proj_body(x_ref, g_ref, w_ref, z_ref):
    h = _rmsnorm(x_ref[...], g_ref[...]).astype(BF16)
    z_ref[...] = jnp.dot(h, w_ref[...], preferred_element_type=F32)


def _in_proj(x, g, w, *, bm):
    t, d = x.shape
    d_in = w.shape[1]
    return pl.pallas_call(
        _in_proj_body,
        grid=(t // bm,),
        in_specs=[
            pl.BlockSpec((bm, d), lambda i: (i, 0)),
            pl.BlockSpec((1, d), lambda i: (0, 0)),
            pl.BlockSpec((d, d_in), lambda i: (0, 0)),
        ],
        out_specs=pl.BlockSpec((bm, d_in), lambda i: (i, 0)),
        out_shape=jax.ShapeDtypeStruct((t, d_in), F32),
        compiler_params=pltpu.CompilerParams(
            dimension_semantics=("arbitrary",),
            vmem_limit_bytes=_vmem_limit(2 * (bm * d * 4 + d * d_in * 2 + bm * d_in * 4) + 16 * 2**20)),
        name="in_proj",
    )(x, g, w)


def _mixer_body(z_ref, zp_ref, zn_ref, x_ref, pw_ref, ps_ref, ng_ref, sw_ref, sb_ref, wo_ref,
                o_ref, ext_ref, vn_ref, y_ref, *, seq):
    i = pl.program_id(0)
    bm = z_ref.shape[0]
    d_pool = ps_ref.shape[1]
    d_sgu = ng_ref.shape[1]
    pg = d_pool // len(POOL_WINDOWS)
    n_heads = sw_ref.shape[0]
    hd = d_sgu // n_heads

    ext_ref[0:HALO, :] = jnp.where(i > 0, zp_ref[...], 0.0)
    ext_ref[HALO:HALO + bm, :] = z_ref[:, 0:d_pool]
    ext_ref[HALO + bm:2 * HALO + bm, :] = jnp.where(i < pl.num_programs(0) - 1, zn_ref[...], 0.0)
    t = i * bm + lax.broadcasted_iota(jnp.int32, (bm, 1), 0)
    for g, w in enumerate(POOL_WINDOWS):
        cs = slice(g * pg, (g + 1) * pg)
        first = HALO - w // 2
        acc = ext_ref[first:first + bm, cs]
        for j in range(1, w):
            acc = acc + ext_ref[first + j:first + j + bm, cs]
        cnt = jnp.minimum(t + (w - w // 2), seq) - jnp.maximum(t - w // 2, 0)
        diff = acc * (1.0 / cnt.astype(F32)) - z_ref[:, cs]
        yp = jnp.dot(diff.astype(BF16), pw_ref[g], preferred_element_type=F32) * ps_ref[:, cs]
        y_ref[:, cs] = yp.astype(BF16)

    v = _gelu(z_ref[:, d_pool + d_sgu:d_pool + 2 * d_sgu])
    mu = jnp.mean(v, axis=-1, keepdims=True)
    vc = v - mu
    var = jnp.mean(vc * vc, axis=-1, keepdims=True)
    vn_ref[...] = (vc * lax.rsqrt(var + LN_EPS) * ng_ref[...]).astype(BF16)
    for c in range(bm // SGU_CHUNK):
        rs = slice(c * SGU_CHUNK, (c + 1) * SGU_CHUNK)
        for h in range(n_heads):
            hs = slice(h * hd, (h + 1) * hd)
            mixed = jnp.dot(sw_ref[h], vn_ref[rs, hs], preferred_element_type=F32) + sb_ref[:, hs]
            u = _gelu(z_ref[rs, d_pool + h * hd:d_pool + (h + 1) * hd])
            y_ref[rs, d_pool + h * hd:d_pool + (h + 1) * hd] = (u * mixed).astype(BF16)

    o_ref[...] = x_ref[...] + jnp.dot(y_ref[...], wo_ref[...], preferred_element_type=F32)


def _mixer(z, x, pool_w, pool_scale, norm_g, sgu_w, sgu_bias, w_out, *, bm):
    t, d = x.shape
    d_in = z.shape[1]
    d_pool = pool_scale.shape[1]
    d_sgu = norm_g.shape[1]
    d_mix = d_pool + d_sgu
    nh = t // HALO
    const2 = lambda i: (0, 0)
    const3 = lambda i: (0, 0, 0)
    return pl.pallas_call(
        functools.partial(_mixer_body, seq=t),
        grid=(t // bm,),
        in_specs=[
            pl.BlockSpec((bm, d_in), lambda i: (i, 0)),
            pl.BlockSpec((HALO, d_pool), lambda i: (jnp.maximum(i * (bm // HALO) - 1, 0), 0)),
            pl.BlockSpec((HALO, d_pool), lambda i: (jnp.minimum((i + 1) * (bm // HALO), nh - 1), 0)),
            pl.BlockSpec((bm, d), lambda i: (i, 0)),
            pl.BlockSpec(pool_w.shape, const3),
            pl.BlockSpec((1, d_pool), const2),
            pl.BlockSpec((1, d_sgu), const2),
            pl.BlockSpec(sgu_w.shape, const3),
            pl.BlockSpec(sgu_bias.shape, const2),
            pl.BlockSpec((d_mix, d), const2),
        ],
        out_specs=pl.BlockSpec((bm, d), lambda i: (i, 0)),
        out_shape=jax.ShapeDtypeStruct((t, d), F32),
        scratch_shapes=[
            pltpu.VMEM((bm + 2 * HALO, d_pool), F32),
            pltpu.VMEM((bm, d_sgu), BF16),
            pltpu.VMEM((bm, d_mix), BF16),
        ],
        compiler_params=pltpu.CompilerParams(
            dimension_semantics=("arbitrary",),
            vmem_limit_bytes=_vmem_limit(2 * (bm * d_in * 4 + 2 * bm * d * 4 + d_mix * d * 2) + 20 * 2**20)),
        name="mixer",
    )(z, z, z, x, pool_w, pool_scale, norm_g, sgu_w, sgu_bias, w_out)


N_EXPERT_MATS = 3


def _dense_ffn_body(x_ref, g_ref, wg_ref, wu_ref, wd_ref, *rest, half, rounding):
    if rounding:
        slabs_in, rest = rest[:N_EXPERT_MATS], rest[N_EXPERT_MATS:]
        o_ref, slabs_out, h_ref = rest[0], rest[1:1 + N_EXPERT_MATS], rest[-1]
        for src, dst in zip(slabs_in, slabs_out):
            dst[...] = src[...].astype(BF16)
    else:
        o_ref, h_ref = rest

    @pl.when(pl.program_id(1) == 0)
    def _():
        x = x_ref[...]
        h_ref[...] = _rmsnorm(x, g_ref[...]).astype(BF16)
        o_ref[...] = x

    for j in range(h_ref.shape[0] // half):
        rows = pl.ds(j * half, half)
        h = h_ref[rows, :]
        gate = jnp.dot(h, wg_ref[...], preferred_element_type=F32)
        up = jnp.dot(h, wu_ref[...], preferred_element_type=F32)
        act = (_silu(gate) * up).astype(BF16)
        o_ref[rows, :] += jnp.dot(act, wd_ref[...], preferred_element_type=F32)


def _expert_slab_split(n_row_blocks, n_experts, d):
    per = n_row_blocks // n_experts
    if per == 0 or n_row_blocks % n_experts or d % per or (d // per) % V7X_LANES:
        return 0
    return per


def _dense_ffn(x, g, w_gate, w_up, w_down, expert_mats=None, *, layer=0, bm, half, tf):
    t, d = x.shape
    d_ff = w_gate.shape[1]
    n_m = t // bm
    in_specs = [
        pl.BlockSpec((bm, d), lambda i, f: (i, 0)),
        pl.BlockSpec((1, d), lambda i, f: (0, 0)),
        pl.BlockSpec((d, tf), lambda i, f: (0, f)),
        pl.BlockSpec((d, tf), lambda i, f: (0, f)),
        pl.BlockSpec((tf, d), lambda i, f: (f, 0)),
    ]
    args = [x, g, w_gate, w_up, w_down]
    out_specs = [pl.BlockSpec((bm, d), lambda i, f: (i, 0))]
    out_shape = [jax.ShapeDtypeStruct((t, d), F32)]
    nbytes = 2 * (2 * bm * d * 4 + 3 * d * tf * 2) + bm * d * 2 + 3 * half * tf * 4
    rounding = expert_mats is not None
    if rounding:
        n_experts = expert_mats[0].shape[1]
        per = _expert_slab_split(n_m, n_experts, d)
        assert per, "the dense grid cannot walk the expert matrices once"
        rb = d // per
        up_like = pl.BlockSpec((None, None, rb, tf), lambda i, f: (layer, i // per, i % per, f))
        down_like = pl.BlockSpec((None, None, tf, rb), lambda i, f: (layer, i // per, f, i % per))
        in_specs += [up_like, up_like, down_like]
        args += list(expert_mats)
        out_specs += [pl.BlockSpec((None, rb, tf), lambda i, f: (i // per, i % per, f))] * 2
        out_specs += [pl.BlockSpec((None, tf, rb), lambda i, f: (i // per, f, i % per))]
        out_shape += [jax.ShapeDtypeStruct((n_experts, d, d_ff), BF16)] * 2
        out_shape += [jax.ShapeDtypeStruct((n_experts, d_ff, d), BF16)]
        nbytes += 2 * N_EXPERT_MATS * rb * tf * (4 + 2)
    res = pl.pallas_call(
        functools.partial(_dense_ffn_body, half=half, rounding=rounding),
        grid=(n_m, d_ff // tf),
        in_specs=in_specs,
        out_specs=out_specs,
        out_shape=out_shape,
        scratch_shapes=[pltpu.VMEM((bm, d), BF16)],
        compiler_params=pltpu.CompilerParams(
            dimension_semantics=("arbitrary", "arbitrary"), vmem_limit_bytes=_vmem_limit(nbytes + 8 * 2**20)),
        name="dense_ffn",
    )(*args)
    return (res[0], tuple(res[1:])) if rounding else res[0]


def _router_body(x_ref, g_ref, rw_ref, hp_ref, ri_ref, rg_ref, cnt_ref, run_ref, *, n_experts):
    i = pl.program_id(0)
    bm, d = x_ref.shape

    @pl.when(i == 0)
    def _():
        run_ref[...] = jnp.zeros_like(run_ref)

    h = _rmsnorm(x_ref[...], g_ref[...])

    h_hi = h.astype(BF16)
    bits = pltpu.bitcast(h_hi.astype(F32), jnp.uint32)
    hp_ref[...] = (bits[:, 0:d // 2] >> 16) | (bits[:, d // 2:d] & jnp.uint32(0xFFFF0000))

    h_lo = (h - h_hi.astype(F32)).astype(BF16)
    rw = rw_ref[...]
    rw_hi = rw.astype(BF16)
    rw_lo = (rw - rw_hi.astype(F32)).astype(BF16)
    logits = (jnp.dot(h_hi, rw_hi, preferred_element_type=F32)
              + (jnp.dot(h_lo, rw_hi, preferred_element_type=F32) + jnp.dot(h_hi, rw_lo, preferred_element_type=F32)))
    lane = lax.broadcasted_iota(jnp.int32, (bm, V7X_LANES), 1)
    neg = jnp.float32(-jnp.inf)
    lg = jnp.where(lane < n_experts, logits, neg)
    m1 = jnp.max(lg, axis=-1, keepdims=True)
    e1 = jnp.min(jnp.where(lg == m1, lane, V7X_LANES), axis=-1, keepdims=True)
    lg2 = jnp.where(lane == e1, neg, lg)
    m2 = jnp.max(lg2, axis=-1, keepdims=True)
    e2 = jnp.min(jnp.where(lg2 == m2, lane, V7X_LANES), axis=-1, keepdims=True)
    ex = jnp.exp(m2 - m1)
    g1 = 1.0 / (1.0 + ex)
    g2 = ex / (1.0 + ex)

    sel = jnp.where((lane == e1) | (lane == e2), 1.0, 0.0)
    r = lax.broadcasted_iota(jnp.int32, (bm, bm), 0)
    c = lax.broadcasted_iota(jnp.int32, (bm, bm), 1)
    tri = jnp.where(c < r, 1.0, 0.0).astype(BF16)
    prefix = jnp.dot(tri, sel.astype(BF16), preferred_element_type=F32) + run_ref[0:1, :]
    rank1 = jnp.sum(jnp.where(lane == e1, prefix, 0.0), axis=-1, keepdims=True).astype(jnp.int32)
    rank2 = jnp.sum(jnp.where(lane == e2, prefix, 0.0), axis=-1, keepdims=True).astype(jnp.int32)
    run_ref[0:1, :] = run_ref[0:1, :] + jnp.sum(sel, axis=0, keepdims=True)

    ri_ref[...] = jnp.where(lane == 0, e1, jnp.where(lane == 1, e2,
                            jnp.where(lane == 2, rank1, jnp.where(lane == 3, rank2, 0))))
    rg_ref[...] = jnp.where(lane == 0, g1, jnp.where(lane == 1, g2, 0.0))
    cnt_ref[...] = run_ref[...].astype(jnp.int32)


def _router(x, g, router_w_padded, *, bm, n_experts):
    t, d = x.shape
    return pl.pallas_call(
        functools.partial(_router_body, n_experts=n_experts),
        grid=(t // bm,),
        in_specs=[
            pl.BlockSpec((bm, d), lambda i: (i, 0)),
            pl.BlockSpec((1, d), lambda i: (0, 0)),
            pl.BlockSpec((d, V7X_LANES), lambda i: (0, 0)),
        ],
        out_specs=[
            pl.BlockSpec((bm, d // 2), lambda i: (i, 0)),
            pl.BlockSpec((bm, V7X_LANES), lambda i: (i, 0)),
            pl.BlockSpec((bm, V7X_LANES), lambda i: (i, 0)),
            pl.BlockSpec((V7X_SUBLANES, V7X_LANES), lambda i: (0, 0)),
        ],
        out_shape=[
            jax.ShapeDtypeStruct((t, d // 2), jnp.uint32),
            jax.ShapeDtypeStruct((t, V7X_LANES), jnp.int32),
            jax.ShapeDtypeStruct((t, V7X_LANES), F32),
            jax.ShapeDtypeStruct((V7X_SUBLANES, V7X_LANES), jnp.int32),
        ],
        scratch_shapes=[pltpu.VMEM((V7X_SUBLANES, V7X_LANES), F32)],
        compiler_params=pltpu.CompilerParams(dimension_semantics=("arbitrary",)),
        name="router",
    )(x, g, router_w_padded)


def _row_copy(src_hbm, dst_ref, sem, src_row, dst_row):
    return pltpu.make_async_copy(src_hbm.at[pl.ds(src_row, 1)], dst_ref.at[pl.ds(dst_row, 1)], sem)


def _expert_body(exp_ref, nv_ref, tok_ref, tokn_ref, hp_hbm, wg_ref, wu_ref, wd_ref, ys_hbm,
                 acc_ref, xw_ref, xs_ref, gsem, wsem, *, sub, half, rpf, n_gather):
    s = pl.program_id(0)
    f = pl.program_id(1)
    ns = pl.num_programs(0)
    nf = pl.num_programs(1)
    sb, d = acc_ref.shape
    nsub = sb // sub
    nvalid = nv_ref[s]
    slot = s % 2

    def writeback(blk):
        return pltpu.make_async_copy(acc_ref, ys_hbm.at[pl.ds(blk * sb, sb)], wsem)

    def gather(tokens_ref, buf, r):
        return _row_copy(hp_hbm, xw_ref.at[buf], gsem.at[buf], tokens_ref[jnp.minimum(r, sb - 1)], r)

    def drain(buf):
        def body(r, carry):
            _row_copy(hp_hbm, xw_ref.at[buf], gsem.at[buf], 0, r).wait()
            return carry
        lax.fori_loop(0, n_gather, body, 0, unroll=8)

    @pl.when(f == 0)
    def _():
        @pl.when(s == 0)
        def _():
            def issue(r, carry):
                gather(tok_ref, slot, r).start()
                return carry
            lax.fori_loop(0, n_gather, issue, 0, unroll=8)

        @pl.when(s > 0)
        def _():
            writeback(s - 1).wait()
        acc_ref[...] = jnp.zeros_like(acc_ref)

        drain(slot)
        for j in range(nsub):
            rows = pl.ds(j * sub, sub)

            @pl.when(j * sub < nvalid)
            def _():
                w = xw_ref[slot, rows, :]
                xs_ref[rows, 0:d // 2] = pltpu.bitcast(w << 16, F32).astype(BF16)
                xs_ref[rows, d // 2:d] = pltpu.bitcast(w & jnp.uint32(0xFFFF0000), F32).astype(BF16)

    def ffn(rows):
        x = xs_ref[rows, :]
        gate = jnp.dot(x, wg_ref[...], preferred_element_type=F32)
        up = jnp.dot(x, wu_ref[...], preferred_element_type=F32)
        act = (_silu(gate) * up).astype(BF16)
        acc_ref[rows, :] += jnp.dot(act, wd_ref[...], preferred_element_type=F32)

    full = nvalid > (nsub - 1) * sub
    n_halves = sb // half

    @pl.when(full)
    def _():
        for j in range(n_halves):
            if j == n_halves - 1:
                for k in range(rpf):
                    gather(tokn_ref, 1 - slot, f * rpf + k).start(priority=1)
            ffn(pl.ds(j * half, half))

    @pl.when(jnp.logical_not(full))
    def _():
        def issue(k, carry):
            gather(tokn_ref, 1 - slot, f * rpf + k).start(priority=1)
            return carry
        lax.fori_loop(0, rpf, issue, 0, unroll=8)

    @pl.when(jnp.logical_and(nvalid > 0, jnp.logical_not(full)))
    def _():
        for j in range(nsub - 1):
            @pl.when(j * sub < nvalid)
            def _():
                ffn(pl.ds(j * sub, sub))

    @pl.when(f == nf - 1)
    def _():
        writeback(s).start()

        @pl.when(s == ns - 1)
        def _():
            writeback(s).wait()
            drain(1 - slot)


def _experts(sb_expert, sb_nvalid, slot_tok, h_packed, w_gate, w_up, w_down, *, sb, sub, half, tf):
    n_sb = sb_expert.shape[0]
    _, d, d_ff = w_gate.shape
    nf = d_ff // tf
    rpf = pl.cdiv(sb, nf)
    n_gather = rpf * nf

    def f_idx(s, f, nv_ref):
        return jnp.where(nv_ref[s] > 0, f, nf - 1)

    grid_spec = pltpu.PrefetchScalarGridSpec(
        num_scalar_prefetch=2,
        grid=(n_sb, nf),
        in_specs=[
            pl.BlockSpec((sb,), lambda s, f, e, nv: (s,), memory_space=pltpu.SMEM),
            pl.BlockSpec((sb,), lambda s, f, e, nv: (jnp.minimum(s + 1, n_sb - 1),), memory_space=pltpu.SMEM),
            pl.BlockSpec(memory_space=pl.ANY),
            pl.BlockSpec((None, d, tf), lambda s, f, e, nv: (e[s], 0, f_idx(s, f, nv))),
            pl.BlockSpec((None, d, tf), lambda s, f, e, nv: (e[s], 0, f_idx(s, f, nv))),
            pl.BlockSpec((None, tf, d), lambda s, f, e, nv: (e[s], f_idx(s, f, nv), 0)),
        ],
        out_specs=pl.BlockSpec(memory_space=pl.ANY),
        scratch_shapes=[
            pltpu.VMEM((sb, d), F32),
            pltpu.VMEM((2, pl.cdiv(n_gather, V7X_SUBLANES) * V7X_SUBLANES, d // 2), jnp.uint32),
            pltpu.VMEM((sb, d), BF16),
            pltpu.SemaphoreType.DMA((2,)),
            pltpu.SemaphoreType.DMA,
        ],
    )
    return pl.pallas_call(
        functools.partial(_expert_body, sub=sub, half=half, rpf=rpf, n_gather=n_gather),
        grid_spec=grid_spec,
        out_shape=jax.ShapeDtypeStruct((n_sb * sb, d), F32),
        compiler_params=pltpu.CompilerParams(
            dimension_semantics=("arbitrary", "arbitrary"),
            vmem_limit_bytes=_vmem_limit(2 * 3 * d * tf * 2 + sb * d * (4 + 2)
                                         + 2 * n_gather * d * 2 + 4 * half * tf * 4 + 8 * 2**20)),
        name="experts",
    )(sb_expert, sb_nvalid, slot_tok, slot_tok, h_packed, w_gate, w_up, w_down)


def _combine_body(dest_ref, destn_ref, rg_ref, x_ref, ys_hbm, g_ref, *rest, proj):
    if proj:
        w_ref, xo_ref, z_ref, y_ref, sem = rest
    else:
        o_ref, y_ref, sem = rest
    i = pl.program_id(0)
    bt = x_ref.shape[0]
    slot = i % 2

    def fetch(d_ref, buf, r, k):
        return _row_copy(ys_hbm, y_ref.at[buf, k], sem.at[buf], d_ref[TOP_K * r + k], r)

    def drain(buf):
        def body(r, carry):
            for k in range(TOP_K):
                _row_copy(ys_hbm, y_ref.at[buf, k], sem.at[buf], 0, r).wait()
            return carry
        lax.fori_loop(0, bt, body, 0, unroll=8)

    @pl.when(i == 0)
    def _():
        def issue(r, carry):
            for k in range(TOP_K):
                fetch(dest_ref, slot, r, k).start()
            return carry
        lax.fori_loop(0, bt, issue, 0, unroll=8)

    drain(slot)
    out = x_ref[...] + (rg_ref[:, 0:1] * y_ref[slot, 0] + rg_ref[:, 1:2] * y_ref[slot, 1])
    if proj:
        xo_ref[...] = out
    for r in range(bt):
        for k in range(TOP_K):
            fetch(destn_ref, 1 - slot, r, k).start(priority=1)
    h = _rmsnorm(out, g_ref[...])
    if proj:
        z_ref[...] = jnp.dot(h.astype(BF16), w_ref[...], preferred_element_type=F32)
    else:
        o_ref[...] = h

    @pl.when(i == pl.num_programs(0) - 1)
    def _():
        drain(1 - slot)


def _combine(dest, rg, x, ys, g, w=None, *, bt):
    t, d = x.shape
    n = t // bt
    proj = w is not None
    in_specs = [
        pl.BlockSpec((TOP_K * bt,), lambda i: (i,), memory_space=pltpu.SMEM),
        pl.BlockSpec((TOP_K * bt,), lambda i: (jnp.minimum(i + 1, n - 1),), memory_space=pltpu.SMEM),
        pl.BlockSpec((bt, V7X_LANES), lambda i: (i, 0)),
        pl.BlockSpec((bt, d), lambda i: (i, 0)),
        pl.BlockSpec(memory_space=pl.ANY),
        pl.BlockSpec((1, d), lambda i: (0, 0)),
    ]
    args = [dest, dest, rg, x, ys, g]
    out_specs = [pl.BlockSpec((bt, d), lambda i: (i, 0))]
    out_shape = [jax.ShapeDtypeStruct((t, d), F32)]
    nbytes = 2 * 2 * bt * d * 4 + 2 * TOP_K * bt * d * 4
    if proj:
        d_in = w.shape[1]
        in_specs.append(pl.BlockSpec((d, d_in), lambda i: (0, 0)))
        args.append(w)
        out_specs.append(pl.BlockSpec((bt, d_in), lambda i: (i, 0)))
        out_shape.append(jax.ShapeDtypeStruct((t, d_in), F32))
        nbytes += 2 * (bt * d_in * 4 + d * d_in * 2)
    res = pl.pallas_call(
        functools.partial(_combine_body, proj=proj),
        grid=(n,),
        in_specs=in_specs,
        out_specs=out_specs,
        out_shape=out_shape,
        scratch_shapes=[pltpu.VMEM((2, TOP_K, bt, d), F32), pltpu.SemaphoreType.DMA((2,))],
        compiler_params=pltpu.CompilerParams(
            dimension_semantics=("arbitrary",), vmem_limit_bytes=_vmem_limit(nbytes + 8 * 2**20)),
        name="combine",
    )(*args)
    return tuple(res) if proj else res[0]


def _routing_tables(ri, cnt, *, n_experts, sb, n_sb):
    t = ri.shape[0]
    experts = ri[:, 0:TOP_K]
    ranks = ri[:, TOP_K:2 * TOP_K]
    counts = cnt[0, 0:n_experts]
    nsb_e = (counts + sb - 1) // sb
    sb_end = jnp.cumsum(nsb_e)
    sb_start = sb_end - nsb_e
    dest = (sb_start * sb)[experts] + ranks
    tok = jnp.broadcast_to(jnp.arange(t, dtype=jnp.int32)[:, None], (t, TOP_K))
    slot_tok = jnp.zeros((n_sb * sb,), jnp.int32).at[dest.reshape(-1)].set(tok.reshape(-1))
    ids = jnp.arange(n_sb, dtype=jnp.int32)
    used = ids < sb_end[-1]
    owner_of = jnp.where(used, ids, jnp.maximum(sb_end[-1] - 1, 0))
    sb_expert = jnp.minimum(jnp.sum(sb_end[None, :] <= owner_of[:, None], axis=1), n_experts - 1).astype(jnp.int32)
    nvalid = jnp.clip(counts[sb_expert] - (ids - sb_start[sb_expert]) * sb, 0, sb)
    sb_nvalid = jnp.where(used, nvalid, 0).astype(jnp.int32)
    return sb_expert, sb_nvalid, slot_tok, dest.reshape(-1).astype(jnp.int32)


def _block(n, pref, mult):
    if n <= pref:
        return n
    b = (pref // mult) * mult
    while n % b:
        b -= mult
    return b


def _forward(x, norm_mix_g, w_in, pool_w, pool_scale, sgu_norm_g, sgu_w, sgu_b, w_out, norm_ffn_g,
             ffn_w_gate, ffn_w_up, ffn_w_down, router_w, moe_w_gate, moe_w_up, moe_w_down,
             final_norm_g, *, bm_mix=512, bm_ffn=512, tf_dense=256, tf_moe=1024, sb=1024, sub=256, half=512,
             bt=256):
    b, s, d = x.shape
    assert b == 1, "pooling windows and gating chunks are laid out for a single sequence"
    t = b * s
    depth = w_in.shape[0]
    n_experts = router_w.shape[-1]
    n_heads, chunk = sgu_b.shape[1:]
    d_sgu = sgu_norm_g.shape[1]
    assert chunk == SGU_CHUNK and depth % 2 == 0 and n_experts <= V7X_LANES
    bm_mix = _block(t, bm_mix, SGU_CHUNK)
    bm_ffn = _block(t, bm_ffn, V7X_SUBLANES)
    bt = _block(t, bt, V7X_SUBLANES)
    sb = min(sb, t)
    sub = min(sub, sb)
    half = min(half, sb)
    bm_dense = _block(t, sb, half)
    tf_dense = _block(ffn_w_gate.shape[-1], tf_dense, V7X_LANES)
    tf_moe = _block(moe_w_gate.shape[-1], tf_moe, V7X_LANES)
    n_sb = (t * TOP_K + n_experts * (sb - 1)) // sb

    x = x.reshape(t, d)
    row = lambda a: a.reshape(1, -1)
    z = None
    for l in range(depth):
        if z is None:
            z = _in_proj(x, row(norm_mix_g[l]), w_in[l].astype(BF16), bm=bm_mix)
        bias = jnp.broadcast_to(sgu_b[l].T[:, :, None], (chunk, n_heads, d_sgu // n_heads)).reshape(chunk, d_sgu)
        x = _mixer(z, x, pool_w[l].astype(BF16), row(pool_scale[l]), row(sgu_norm_g[l]),
                   sgu_w[l].astype(BF16), bias, w_out[l].astype(BF16), bm=bm_mix)
        z = None
        i = l // 2
        if l % 2 == 0:
            dense_w = (ffn_w_gate[i].astype(BF16), ffn_w_up[i].astype(BF16), ffn_w_down[i].astype(BF16))
            if _expert_slab_split(t // bm_dense, n_experts, d):
                x, expert_w = _dense_ffn(x, row(norm_ffn_g[l]), *dense_w, (moe_w_gate, moe_w_up, moe_w_down),
                                         layer=i, bm=bm_dense, half=half, tf=tf_dense)
            else:
                x = _dense_ffn(x, row(norm_ffn_g[l]), *dense_w, bm=bm_dense, half=half, tf=tf_dense)
                expert_w = (moe_w_gate[i].astype(BF16), moe_w_up[i].astype(BF16), moe_w_down[i].astype(BF16))
        else:
            rw = jnp.pad(router_w[i], ((0, 0), (0, V7X_LANES - n_experts)))
            h_packed, ri, rg, cnt = _router(x, row(norm_ffn_g[l]), rw, bm=bm_ffn, n_experts=n_experts)
            sb_expert, sb_nvalid, slot_tok, dest = _routing_tables(ri, cnt, n_experts=n_experts, sb=sb, n_sb=n_sb)
            ys = _experts(sb_expert, sb_nvalid, slot_tok, h_packed, *expert_w, sb=sb, sub=sub, half=half, tf=tf_moe)
            if l == depth - 1:
                x = _combine(dest, rg, x, ys, row(final_norm_g), bt=bt)
            else:
                x, z = _combine(dest, rg, x, ys, row(norm_mix_g[l + 1]), w_in[l + 1].astype(BF16), bt=bt)
    return x.reshape(b, s, d)


def kernel(x, norm_mix_g, w_in, pool_w, pool_scale, sgu_norm_g, sgu_w, sgu_b, w_out, norm_ffn_g,
           ffn_w_gate, ffn_w_up, ffn_w_down, router_w, moe_w_gate, moe_w_up, moe_w_down, final_norm_g):
    return _forward(x, norm_mix_g, w_in, pool_w, pool_scale, sgu_norm_g, sgu_w, sgu_b, w_out, norm_ffn_g,
                    ffn_w_gate, ffn_w_up, ffn_w_down, router_w, moe_w_gate, moe_w_up, moe_w_down,
                    final_norm_g)
```

```python
import functools

import jax
import jax.numpy as jnp
from jax import lax
from jax.experimental import pallas as pl
from jax.experimental.pallas import tpu as pltpu

RMS_EPS = 1e-6
LN_EPS = 1e-5
POOL_WINDOWS = (2, 4, 8, 16)
SGU_CHUNK = 128
TOP_K = 2

V7X_LANES = 128
V7X_SUBLANES = 8
V7X_VMEM_BYTES = 64 * 1024 * 1024
HALO = max(POOL_WINDOWS) // 2
assert HALO == V7X_SUBLANES

BF16 = jnp.bfloat16
F32 = jnp.float32


def _vmem_limit(nbytes):
    return int(min(nbytes, V7X_VMEM_BYTES - 6 * 1024 * 1024))


def _rmsnorm(x, g):
    return x * lax.rsqrt(jnp.mean(x * x, axis=-1, keepdims=True) + RMS_EPS) * g


def _gelu(x):
    return 0.5 * x * (1.0 + lax.erf(x * (2.0 ** -0.5)))


def _silu(x):
    return x / (1.0 + jnp.exp(-x))


def _in_proj_body(x_ref, g_ref, w_ref, z_ref):
    h = _rmsnorm(x_ref[...], g_ref[...]).astype(BF16)
    z_ref[...] = jnp.dot(h, w_ref[...], preferred_element_type=F32)


def _in_proj(x, g, w, *, bm):
    t, d = x.shape
    d_in = w.shape[1]
    return pl.pallas_call(
        _in_proj_body,
        grid=(t // bm,),
        in_specs=[
            pl.BlockSpec((bm, d), lambda i: (i, 0)),
            pl.BlockSpec((1, d), lambda i: (0, 0)),
            pl.BlockSpec((d, d_in), lambda i: (0, 0)),
        ],
        out_specs=pl.BlockSpec((bm, d_in), lambda i: (i, 0)),
        out_shape=jax.ShapeDtypeStruct((t, d_in), F32),
        compiler_params=pltpu.CompilerParams(
            dimension_semantics=("arbitrary",),
            vmem_limit_bytes=_vmem_limit(2 * (bm * d * 4 + d * d_in * 2 + bm * d_in * 4) + 16 * 2**20)),
        name="in_proj",
    )(x, g, w)


def _mixer_body(z_ref, zp_ref, zn_ref, x_ref, pw_ref, ps_ref, ng_ref, sw_ref, sb_ref, wo_ref,
                o_ref, ext_ref, vn_ref, y_ref, *, seq):
    i = pl.program_id(0)
    bm = z_ref.shape[0]
    d_pool = ps_ref.shape[1]
    d_sgu = ng_ref.shape[1]
    pg = d_pool // len(POOL_WINDOWS)
    n_heads = sw_ref.shape[0]
    hd = d_sgu // n_heads

    ext_ref[0:HALO, :] = jnp.where(i > 0, zp_ref[...], 0.0)
    ext_ref[HALO:HALO + bm, :] = z_ref[:, 0:d_pool]
    ext_ref[HALO + bm:2 * HALO + bm, :] = jnp.where(i < pl.num_programs(0) - 1, zn_ref[...], 0.0)
    t = i * bm + lax.broadcasted_iota(jnp.int32, (bm, 1), 0)
    for g, w in enumerate(POOL_WINDOWS):
        cs = slice(g * pg, (g + 1) * pg)
        first = HALO - w // 2
        acc = ext_ref[first:first + bm, cs]
        for j in range(1, w):
            acc = acc + ext_ref[first + j:first + j + bm, cs]
        cnt = jnp.minimum(t + (w - w // 2), seq) - jnp.maximum(t - w // 2, 0)
        diff = acc * (1.0 / cnt.astype(F32)) - z_ref[:, cs]
        yp = jnp.dot(diff.astype(BF16), pw_ref[g], preferred_element_type=F32) * ps_ref[:, cs]
        y_ref[:, cs] = yp.astype(BF16)

    v = _gelu(z_ref[:, d_pool + d_sgu:d_pool + 2 * d_sgu])
    mu = jnp.mean(v, axis=-1, keepdims=True)
    vc = v - mu
    var = jnp.mean(vc * vc, axis=-1, keepdims=True)
    vn_ref[...] = (vc * lax.rsqrt(var + LN_EPS) * ng_ref[...]).astype(BF16)
    for c in range(bm // SGU_CHUNK):
        rs = slice(c * SGU_CHUNK, (c + 1) * SGU_CHUNK)
        for h in range(n_heads):
            hs = slice(h * hd, (h + 1) * hd)
            mixed = jnp.dot(sw_ref[h], vn_ref[rs, hs], preferred_element_type=F32) + sb_ref[:, hs]
            u = _gelu(z_ref[rs, d_pool + h * hd:d_pool + (h + 1) * hd])
            y_ref[rs, d_pool + h * hd:d_pool + (h + 1) * hd] = (u * mixed).astype(BF16)

    o_ref[...] = x_ref[...] + jnp.dot(y_ref[...], wo_ref[...], preferred_element_type=F32)


def _mixer(z, x, pool_w, pool_scale, norm_g, sgu_w, sgu_bias, w_out, *, bm):
    t, d = x.shape
    d_in = z.shape[1]
    d_pool = pool_scale.shape[1]
    d_sgu = norm_g.shape[1]
    d_mix = d_pool + d_sgu
    nh = t // HALO
    const2 = lambda i: (0, 0)
    const3 = lambda i: (0, 0, 0)
    return pl.pallas_call(
        functools.partial(_mixer_body, seq=t),
        grid=(t // bm,),
        in_specs=[
            pl.BlockSpec((bm, d_in), lambda i: (i, 0)),
            pl.BlockSpec((HALO, d_pool), lambda i: (jnp.maximum(i * (bm // HALO) - 1, 0), 0)),
            pl.BlockSpec((HALO, d_pool), lambda i: (jnp.minimum((i + 1) * (bm // HALO), nh - 1), 0)),
            pl.BlockSpec((bm, d), lambda i: (i, 0)),
            pl.BlockSpec(pool_w.shape, const3),
            pl.BlockSpec((1, d_pool), const2),
            pl.BlockSpec((1, d_sgu), const2),
            pl.BlockSpec(sgu_w.shape, const3),
            pl.BlockSpec(sgu_bias.shape, const2),
            pl.BlockSpec((d_mix, d), const2),
        ],
        out_specs=pl.BlockSpec((bm, d), lambda i: (i, 0)),
        out_shape=jax.ShapeDtypeStruct((t, d), F32),
        scratch_shapes=[
            pltpu.VMEM((bm + 2 * HALO, d_pool), F32),
            pltpu.VMEM((bm, d_sgu), BF16),
            pltpu.VMEM((bm, d_mix), BF16),
        ],
        compiler_params=pltpu.CompilerParams(
            dimension_semantics=("arbitrary",),
            vmem_limit_bytes=_vmem_limit(2 * (bm * d_in * 4 + 2 * bm * d * 4 + d_mix * d * 2) + 20 * 2**20)),
        name="mixer",
    )(z, z, z, x, pool_w, pool_scale, norm_g, sgu_w, sgu_bias, w_out)


N_EXPERT_MATS = 3


def _dense_ffn_body(x_ref, g_ref, wg_ref, wu_ref, wd_ref, *rest, half, rounding):
    if rounding:
        slabs_in, rest = rest[:N_EXPERT_MATS], rest[N_EXPERT_MATS:]
        o_ref, slabs_out, h_ref = rest[0], rest[1:1 + N_EXPERT_MATS], rest[-1]
        for src, dst in zip(slabs_in, slabs_out):
            dst[...] = src[...].astype(BF16)
    else:
        o_ref, h_ref = rest

    @pl.when(pl.program_id(1) == 0)
    def _():
        x = x_ref[...]
        h_ref[...] = _rmsnorm(x, g_ref[...]).astype(BF16)
        o_ref[...] = x

    for j in range(h_ref.shape[0] // half):
        rows = pl.ds(j * half, half)
        h = h_ref[rows, :]
        gate = jnp.dot(h, wg_ref[...], preferred_element_type=F32)
        up = jnp.dot(h, wu_ref[...], preferred_element_type=F32)
        act = (_silu(gate) * up).astype(BF16)
        o_ref[rows, :] += jnp.dot(act, wd_ref[...], preferred_element_type=F32)


def _expert_slab_split(n_row_blocks, n_experts, d):
    per = n_row_blocks // n_experts
    if per == 0 or n_row_blocks % n_experts or d % per or (d // per) % V7X_LANES:
        return 0
    return per


def _dense_ffn(x, g, w_gate, w_up, w_down, expert_mats=None, *, layer=0, bm, half, tf):
    t, d = x.shape
    d_ff = w_gate.shape[1]
    n_m = t // bm
    in_specs = [
        pl.BlockSpec((bm, d), lambda i, f: (i, 0)),
        pl.BlockSpec((1, d), lambda i, f: (0, 0)),
        pl.BlockSpec((d, tf), lambda i, f: (0, f)),
        pl.BlockSpec((d, tf), lambda i, f: (0, f)),
        pl.BlockSpec((tf, d), lambda i, f: (f, 0)),
    ]
    args = [x, g, w_gate, w_up, w_down]
    out_specs = [pl.BlockSpec((bm, d), lambda i, f: (i, 0))]
    out_shape = [jax.ShapeDtypeStruct((t, d), F32)]
    nbytes = 2 * (2 * bm * d * 4 + 3 * d * tf * 2) + bm * d * 2 + 3 * half * tf * 4
    rounding = expert_mats is not None
    if rounding:
        n_experts = expert_mats[0].shape[1]
        per = _expert_slab_split(n_m, n_experts, d)
        assert per, "the dense grid cannot walk the expert matrices once"
        rb = d // per
        up_like = pl.BlockSpec((None, None, rb, tf), lambda i, f: (layer, i // per, i % per, f))
        down_like = pl.BlockSpec((None, None, tf, rb), lambda i, f: (layer, i // per, f, i % per))
        in_specs += [up_like, up_like, down_like]
        args += list(expert_mats)
        out_specs += [pl.BlockSpec((None, rb, tf), lambda i, f: (i // per, i % per, f))] * 2
        out_specs += [pl.BlockSpec((None, tf, rb), lambda i, f: (i // per, f, i % per))]
        out_shape += [jax.ShapeDtypeStruct((n_experts, d, d_ff), BF16)] * 2
        out_shape += [jax.ShapeDtypeStruct((n_experts, d_ff, d), BF16)]
        nbytes += 2 * N_EXPERT_MATS * rb * tf * (4 + 2)
    res = pl.pallas_call(
        functools.partial(_dense_ffn_body, half=half, rounding=rounding),
        grid=(n_m, d_ff // tf),
        in_specs=in_specs,
        out_specs=out_specs,
        out_shape=out_shape,
        scratch_shapes=[pltpu.VMEM((bm, d), BF16)],
        compiler_params=pltpu.CompilerParams(
            dimension_semantics=("arbitrary", "arbitrary"), vmem_limit_bytes=_vmem_limit(nbytes + 8 * 2**20)),
        name="dense_ffn",
    )(*args)
    return (res[0], tuple(res[1:])) if rounding else res[0]


def _router_body(x_ref, g_ref, rw_ref, hp_ref, ri_ref, rg_ref, cnt_ref, run_ref, *, n_experts):
    i = pl.program_id(0)
    bm, d = x_ref.shape

    @pl.when(i == 0)
    def _():
        run_ref[...] = jnp.zeros_like(run_ref)

    h = _rmsnorm(x_ref[...], g_ref[...])

    h_hi = h.astype(BF16)
    bits = pltpu.bitcast(h_hi.astype(F32), jnp.uint32)
    hp_ref[...] = (bits[:, 0:d // 2] >> 16) | (bits[:, d // 2:d] & jnp.uint32(0xFFFF0000))

    h_lo = (h - h_hi.astype(F32)).astype(BF16)
    rw = rw_ref[...]
    rw_hi = rw.astype(BF16)
    rw_lo = (rw - rw_hi.astype(F32)).astype(BF16)
    logits = (jnp.dot(h_hi, rw_hi, preferred_element_type=F32)
              + (jnp.dot(h_lo, rw_hi, preferred_element_type=F32) + jnp.dot(h_hi, rw_lo, preferred_element_type=F32)))
    lane = lax.broadcasted_iota(jnp.int32, (bm, V7X_LANES), 1)
    neg = jnp.float32(-jnp.inf)
    lg = jnp.where(lane < n_experts, logits, neg)
    m1 = jnp.max(lg, axis=-1, keepdims=True)
    e1 = jnp.min(jnp.where(lg == m1, lane, V7X_LANES), axis=-1, keepdims=True)
    lg2 = jnp.where(lane == e1, neg, lg)
    m2 = jnp.max(lg2, axis=-1, keepdims=True)
    e2 = jnp.min(jnp.where(lg2 == m2, lane, V7X_LANES), axis=-1, keepdims=True)
    ex = jnp.exp(m2 - m1)
    g1 = 1.0 / (1.0 + ex)
    g2 = ex / (1.0 + ex)

    sel = jnp.where((lane == e1) | (lane == e2), 1.0, 0.0)
    r = lax.broadcasted_iota(jnp.int32, (bm, bm), 0)
    c = lax.broadcasted_iota(jnp.int32, (bm, bm), 1)
    tri = jnp.where(c < r, 1.0, 0.0).astype(BF16)
    prefix = jnp.dot(tri, sel.astype(BF16), preferred_element_type=F32) + run_ref[0:1, :]
    rank1 = jnp.sum(jnp.where(lane == e1, prefix, 0.0), axis=-1, keepdims=True).astype(jnp.int32)
    rank2 = jnp.sum(jnp.where(lane == e2, prefix, 0.0), axis=-1, keepdims=True).astype(jnp.int32)
    run_ref[0:1, :] = run_ref[0:1, :] + jnp.sum(sel, axis=0, keepdims=True)

    ri_ref[...] = jnp.where(lane == 0, e1, jnp.where(lane == 1, e2,
                            jnp.where(lane == 2, rank1, jnp.where(lane == 3, rank2, 0))))
    rg_ref[...] = jnp.where(lane == 0, g1, jnp.where(lane == 1, g2, 0.0))
    cnt_ref[...] = run_ref[...].astype(jnp.int32)


def _router(x, g, router_w_padded, *, bm, n_experts):
    t, d = x.shape
    return pl.pallas_call(
        functools.partial(_router_body, n_experts=n_experts),
        grid=(t // bm,),
        in_specs=[
            pl.BlockSpec((bm, d), lambda i: (i, 0)),
            pl.BlockSpec((1, d), lambda i: (0, 0)),
            pl.BlockSpec((d, V7X_LANES), lambda i: (0, 0)),
        ],
        out_specs=[
            pl.BlockSpec((bm, d // 2), lambda i: (i, 0)),
            pl.BlockSpec((bm, V7X_LANES), lambda i: (i, 0)),
            pl.BlockSpec((bm, V7X_LANES), lambda i: (i, 0)),
            pl.BlockSpec((V7X_SUBLANES, V7X_LANES), lambda i: (0, 0)),
        ],
        out_shape=[
            jax.ShapeDtypeStruct((t, d // 2), jnp.uint32),
            jax.ShapeDtypeStruct((t, V7X_LANES), jnp.int32),
            jax.ShapeDtypeStruct((t, V7X_LANES), F32),
            jax.ShapeDtypeStruct((V7X_SUBLANES, V7X_LANES), jnp.int32),
        ],
        scratch_shapes=[pltpu.VMEM((V7X_SUBLANES, V7X_LANES), F32)],
        compiler_params=pltpu.CompilerParams(dimension_semantics=("arbitrary",)),
        name="router",
    )(x, g, router_w_padded)


def _row_copy(src_hbm, dst_ref, sem, src_row, dst_row):
    return pltpu.make_async_copy(src_hbm.at[pl.ds(src_row, 1)], dst_ref.at[pl.ds(dst_row, 1)], sem)


def _expert_body(exp_ref, nv_ref, tok_ref, tokn_ref, hp_hbm, wg_ref, wu_ref, wd_ref, ys_hbm,
                 acc_ref, xw_ref, xs_ref, gsem, wsem, *, sub, half, rpf, n_gather):
    s = pl.program_id(0)
    f = pl.program_id(1)
    ns = pl.num_programs(0)
    nf = pl.num_programs(1)
    sb, d = acc_ref.shape
    nsub = sb // sub
    nvalid = nv_ref[s]
    slot = s % 2

    def writeback(blk):
        return pltpu.make_async_copy(acc_ref, ys_hbm.at[pl.ds(blk * sb, sb)], wsem)

    def gather(tokens_ref, buf, r):
        return _row_copy(hp_hbm, xw_ref.at[buf], gsem.at[buf], tokens_ref[jnp.minimum(r, sb - 1)], r)

    def drain(buf):
        def body(r, carry):
            _row_copy(hp_hbm, xw_ref.at[buf], gsem.at[buf], 0, r).wait()
            return carry
        lax.fori_loop(0, n_gather, body, 0, unroll=8)

    @pl.when(f == 0)
    def _():
        @pl.when(s == 0)
        def _():
            def issue(r, carry):
                gather(tok_ref, slot, r).start()
                return carry
            lax.fori_loop(0, n_gather, issue, 0, unroll=8)

        @pl.when(s > 0)
        def _():
            writeback(s - 1).wait()
        acc_ref[...] = jnp.zeros_like(acc_ref)

        drain(slot)
        for j in range(nsub):
            rows = pl.ds(j * sub, sub)

            @pl.when(j * sub < nvalid)
            def _():
                w = xw_ref[slot, rows, :]
                xs_ref[rows, 0:d // 2] = pltpu.bitcast(w << 16, F32).astype(BF16)
                xs_ref[rows, d // 2:d] = pltpu.bitcast(w & jnp.uint32(0xFFFF0000), F32).astype(BF16)

    def ffn(rows):
        x = xs_ref[rows, :]
        gate = jnp.dot(x, wg_ref[...], preferred_element_type=F32)
        up = jnp.dot(x, wu_ref[...], preferred_element_type=F32)
        act = (_silu(gate) * up).astype(BF16)
        acc_ref[rows, :] += jnp.dot(act, wd_ref[...], preferred_element_type=F32)

    full = nvalid > (nsub - 1) * sub
    n_halves = sb // half

    @pl.when(full)
    def _():
        for j in range(n_halves):
            if j == n_halves - 1:
                for k in range(rpf):
                    gather(tokn_ref, 1 - slot, f * rpf + k).start(priority=1)
            ffn(pl.ds(j * half, half))

    @pl.when(jnp.logical_not(full))
    def _():
        def issue(k, carry):
            gather(tokn_ref, 1 - slot, f * rpf + k).start(priority=1)
            return carry
        lax.fori_loop(0, rpf, issue, 0, unroll=8)

    @pl.when(jnp.logical_and(nvalid > 0, jnp.logical_not(full)))
    def _():
        for j in range(nsub - 1):
            @pl.when(j * sub < nvalid)
            def _():
                ffn(pl.ds(j * sub, sub))

    @pl.when(f == nf - 1)
    def _():
        writeback(s).start()

        @pl.when(s == ns - 1)
        def _():
            writeback(s).wait()
            drain(1 - slot)


def _experts(sb_expert, sb_nvalid, slot_tok, h_packed, w_gate, w_up, w_down, *, sb, sub, half, tf):
    n_sb = sb_expert.shape[0]
    _, d, d_ff = w_gate.shape
    nf = d_ff // tf
    rpf = pl.cdiv(sb, nf)
    n_gather = rpf * nf

    def f_idx(s, f, nv_ref):
        return jnp.where(nv_ref[s] > 0, f, nf - 1)

    grid_spec = pltpu.PrefetchScalarGridSpec(
        num_scalar_prefetch=2,
        grid=(n_sb, nf),
        in_specs=[
            pl.BlockSpec((sb,), lambda s, f, e, nv: (s,), memory_space=pltpu.SMEM),
            pl.BlockSpec((sb,), lambda s, f, e, nv: (jnp.minimum(s + 1, n_sb - 1),), memory_space=pltpu.SMEM),
            pl.BlockSpec(memory_space=pl.ANY),
            pl.BlockSpec((None, d, tf), lambda s, f, e, nv: (e[s], 0, f_idx(s, f, nv))),
            pl.BlockSpec((None, d, tf), lambda s, f, e, nv: (e[s], 0, f_idx(s, f, nv))),
            pl.BlockSpec((None, tf, d), lambda s, f, e, nv: (e[s], f_idx(s, f, nv), 0)),
        ],
        out_specs=pl.BlockSpec(memory_space=pl.ANY),
        scratch_shapes=[
            pltpu.VMEM((sb, d), F32),
            pltpu.VMEM((2, pl.cdiv(n_gather, V7X_SUBLANES) * V7X_SUBLANES, d // 2), jnp.uint32),
            pltpu.VMEM((sb, d), BF16),
            pltpu.SemaphoreType.DMA((2,)),
            pltpu.SemaphoreType.DMA,
        ],
    )
    return pl.pallas_call(
        functools.partial(_expert_body, sub=sub, half=half, rpf=rpf, n_gather=n_gather),
        grid_spec=grid_spec,
        out_shape=jax.ShapeDtypeStruct((n_sb * sb, d), F32),
        compiler_params=pltpu.CompilerParams(
            dimension_semantics=("arbitrary", "arbitrary"),
            vmem_limit_bytes=_vmem_limit(2 * 3 * d * tf * 2 + sb * d * (4 + 2)
                                         + 2 * n_gather * d * 2 + 4 * half * tf * 4 + 8 * 2**20)),
        name="experts",
    )(sb_expert, sb_nvalid, slot_tok, slot_tok, h_packed, w_gate, w_up, w_down)


def _combine_body(dest_ref, destn_ref, rg_ref, x_ref, ys_hbm, g_ref, *rest, proj):
    if proj:
        w_ref, xo_ref, z_ref, y_ref, sem = rest
    else:
        o_ref, y_ref, sem = rest
    i = pl.program_id(0)
    bt = x_ref.shape[0]
    slot = i % 2

    def fetch(d_ref, buf, r, k):
        return _row_copy(ys_hbm, y_ref.at[buf, k], sem.at[buf], d_ref[TOP_K * r + k], r)

    def drain(buf):
        def body(r, carry):
            for k in range(TOP_K):
                _row_copy(ys_hbm, y_ref.at[buf, k], sem.at[buf], 0, r).wait()
            return carry
        lax.fori_loop(0, bt, body, 0, unroll=8)

    @pl.when(i == 0)
    def _():
        def issue(r, carry):
            for k in range(TOP_K):
                fetch(dest_ref, slot, r, k).start()
            return carry
        lax.fori_loop(0, bt, issue, 0, unroll=8)

    drain(slot)
    out = x_ref[...] + (rg_ref[:, 0:1] * y_ref[slot, 0] + rg_ref[:, 1:2] * y_ref[slot, 1])
    if proj:
        xo_ref[...] = out
    for r in range(bt):
        for k in range(TOP_K):
            fetch(destn_ref, 1 - slot, r, k).start(priority=k)
    h = _rmsnorm(out, g_ref[...])
    if proj:
        z_ref[...] = jnp.dot(h.astype(BF16), w_ref[...], preferred_element_type=F32)
    else:
        o_ref[...] = h

    @pl.when(i == pl.num_programs(0) - 1)
    def _():
        drain(1 - slot)


def _combine(dest, rg, x, ys, g, w=None, *, bt):
    t, d = x.shape
    n = t // bt
    proj = w is not None
    in_specs = [
        pl.BlockSpec((TOP_K * bt,), lambda i: (i,), memory_space=pltpu.SMEM),
        pl.BlockSpec((TOP_K * bt,), lambda i: (jnp.minimum(i + 1, n - 1),), memory_space=pltpu.SMEM),
        pl.BlockSpec((bt, V7X_LANES), lambda i: (i, 0)),
        pl.BlockSpec((bt, d), lambda i: (i, 0)),
        pl.BlockSpec(memory_space=pl.ANY),
        pl.BlockSpec((1, d), lambda i: (0, 0)),
    ]
    args = [dest, dest, rg, x, ys, g]
    out_specs = [pl.BlockSpec((bt, d), lambda i: (i, 0))]
    out_shape = [jax.ShapeDtypeStruct((t, d), F32)]
    nbytes = 2 * 2 * bt * d * 4 + 2 * TOP_K * bt * d * 4
    if proj:
        d_in = w.shape[1]
        in_specs.append(pl.BlockSpec((d, d_in), lambda i: (0, 0)))
        args.append(w)
        out_specs.append(pl.BlockSpec((bt, d_in), lambda i: (i, 0)))
        out_shape.append(jax.ShapeDtypeStruct((t, d_in), F32))
        nbytes += 2 * (bt * d_in * 4 + d * d_in * 2)
    res = pl.pallas_call(
        functools.partial(_combine_body, proj=proj),
        grid=(n,),
        in_specs=in_specs,
        out_specs=out_specs,
        out_shape=out_shape,
        scratch_shapes=[pltpu.VMEM((2, TOP_K, bt, d), F32), pltpu.SemaphoreType.DMA((2,))],
        compiler_params=pltpu.CompilerParams(
            dimension_semantics=("arbitrary",), vmem_limit_bytes=_vmem_limit(nbytes + 8 * 2**20)),
        name="combine",
    )(*args)
    return tuple(res) if proj else res[0]


def _routing_tables(ri, cnt, *, n_experts, sb, n_sb):
    t = ri.shape[0]
    experts = ri[:, 0:TOP_K]
    ranks = ri[:, TOP_K:2 * TOP_K]
    counts = cnt[0, 0:n_experts]
    nsb_e = (counts + sb - 1) // sb
    sb_end = jnp.cumsum(nsb_e)
    sb_start = sb_end - nsb_e
    dest = (sb_start * sb)[experts] + ranks
    tok = jnp.broadcast_to(jnp.arange(t, dtype=jnp.int32)[:, None], (t, TOP_K))
    slot_tok = jnp.zeros((n_sb * sb,), jnp.int32).at[dest.reshape(-1)].set(tok.reshape(-1))
    ids = jnp.arange(n_sb, dtype=jnp.int32)
    used = ids < sb_end[-1]
    owner_of = jnp.where(used, ids, jnp.maximum(sb_end[-1] - 1, 0))
    sb_expert = jnp.minimum(jnp.sum(sb_end[None, :] <= owner_of[:, None], axis=1), n_experts - 1).astype(jnp.int32)
    nvalid = jnp.clip(counts[sb_expert] - (ids - sb_start[sb_expert]) * sb, 0, sb)
    sb_nvalid = jnp.where(used, nvalid, 0).astype(jnp.int32)
    return sb_expert, sb_nvalid, slot_tok, dest.reshape(-1).astype(jnp.int32)


def _block(n, pref, mult):
    if n <= pref:
        return n
    b = (pref // mult) * mult
    while n % b:
        b -= mult
    return b


def _forward(x, norm_mix_g, w_in, pool_w, pool_scale, sgu_norm_g, sgu_w, sgu_b, w_out, norm_ffn_g,
             ffn_w_gate, ffn_w_up, ffn_w_down, router_w, moe_w_gate, moe_w_up, moe_w_down,
             final_norm_g, *, bm_mix=512, bm_ffn=512, tf_dense=256, tf_moe=512, sb=1024, sub=256, half=512,
             bt=256):
    b, s, d = x.shape
    assert b == 1, "pooling windows and gating chunks are laid out for a single sequence"
    t = b * s
    depth = w_in.shape[0]
    n_experts = router_w.shape[-1]
    n_heads, chunk = sgu_b.shape[1:]
    d_sgu = sgu_norm_g.shape[1]
    assert chunk == SGU_CHUNK and depth % 2 == 0 and n_experts <= V7X_LANES
    bm_mix = _block(t, bm_mix, SGU_CHUNK)
    bm_ffn = _block(t, bm_ffn, V7X_SUBLANES)
    bt = _block(t, bt, V7X_SUBLANES)
    sb = min(sb, t)
    sub = min(sub, sb)
    half = min(half, sb)
    bm_dense = _block(t, sb, half)
    tf_dense = _block(ffn_w_gate.shape[-1], tf_dense, V7X_LANES)
    tf_moe = _block(moe_w_gate.shape[-1], tf_moe, V7X_LANES)
    n_sb = (t * TOP_K + n_experts * (sb - 1)) // sb

    x = x.reshape(t, d)
    row = lambda a: a.reshape(1, -1)
    z = None
    for l in range(depth):
        if z is None:
            z = _in_proj(x, row(norm_mix_g[l]), w_in[l].astype(BF16), bm=bm_mix)
        bias = jnp.broadcast_to(sgu_b[l].T[:, :, None], (chunk, n_heads, d_sgu // n_heads)).reshape(chunk, d_sgu)
        x = _mixer(z, x, pool_w[l].astype(BF16), row(pool_scale[l]), row(sgu_norm_g[l]),
                   sgu_w[l].astype(BF16), bias, w_out[l].astype(BF16), bm=bm_mix)
        z = None
        i = l // 2
        if l % 2 == 0:
            dense_w = (ffn_w_gate[i].astype(BF16), ffn_w_up[i].astype(BF16), ffn_w_down[i].astype(BF16))
            if _expert_slab_split(t // bm_dense, n_experts, d):
                x, expert_w = _dense_ffn(x, row(norm_ffn_g[l]), *dense_w, (moe_w_gate, moe_w_up, moe_w_down),
                                         layer=i, bm=bm_dense, half=half, tf=tf_dense)
            else:
                x = _dense_ffn(x, row(norm_ffn_g[l]), *dense_w, bm=bm_dense, half=half, tf=tf_dense)
                expert_w = (moe_w_gate[i].astype(BF16), moe_w_up[i].astype(BF16), moe_w_down[i].astype(BF16))
        else:
            rw = jnp.pad(router_w[i], ((0, 0), (0, V7X_LANES - n_experts)))
            h_packed, ri, rg, cnt = _router(x, row(norm_ffn_g[l]), rw, bm=bm_ffn, n_experts=n_experts)
            sb_expert, sb_nvalid, slot_tok, dest = _routing_tables(ri, cnt, n_experts=n_experts, sb=sb, n_sb=n_sb)
            ys = _experts(sb_expert, sb_nvalid, slot_tok, h_packed, *expert_w, sb=sb, sub=sub, half=sb, tf=tf_moe)
            if l == depth - 1:
                x = _combine(dest, rg, x, ys, row(final_norm_g), bt=bt)
            else:
                x, z = _combine(dest, rg, x, ys, row(norm_mix_g[l + 1]), w_in[l + 1].astype(BF16), bt=bt)
    return x.reshape(b, s, d)


def kernel(x, norm_mix_g, w_in, pool_w, pool_scale, sgu_norm_g, sgu_w, sgu_b, w_out, norm_ffn_g,
           ffn_w_gate, ffn_w_up, ffn_w_down, router_w, moe_w_gate, moe_w_up, moe_w_down, final_norm_g):
    return _forward(x, norm_mix_g, w_in, pool_w, pool_scale, sgu_norm_g, sgu_w, sgu_b, w_out, norm_ffn_g,
                    ffn_w_gate, ffn_w_up, ffn_w_down, router_w, moe_w_gate, moe_w_up, moe_w_down,
                    final_norm_g)
```

```python
import functools

import jax
import jax.numpy as jnp
from jax import lax
from jax.experimental import pallas as pl
from jax.experimental.pallas import tpu as pltpu

RMS_EPS = 1e-6
LN_EPS = 1e-5
POOL_WINDOWS = (2, 4, 8, 16)
SGU_CHUNK = 128
TOP_K = 2

V7X_LANES = 128
V7X_SUBLANES = 8
V7X_VMEM_BYTES = 64 * 1024 * 1024
HALO = max(POOL_WINDOWS) // 2
assert HALO == V7X_SUBLANES

BF16 = jnp.bfloat16
F32 = jnp.float32


def _vmem_limit(nbytes):
    return int(min(nbytes, V7X_VMEM_BYTES - 6 * 1024 * 1024))


def _rmsnorm(x, g):
    return x * lax.rsqrt(jnp.mean(x * x, axis=-1, keepdims=True) + RMS_EPS) * g


def _gelu(x):
    return 0.5 * x * (1.0 + lax.erf(x * (2.0 ** -0.5)))


def _silu(x):
    return x / (1.0 + jnp.exp(-x))


def _in_proj_body(x_ref, g_ref, w_ref, z_ref):
    h = _rmsnorm(x_ref[...], g_ref[...]).astype(BF16)
    z_ref[...] = jnp.dot(h, w_ref[...], preferred_element_type=F32)


def _in_proj(x, g, w, *, bm):
    t, d = x.shape
    d_in = w.shape[1]
    return pl.pallas_call(
        _in_proj_body,
        grid=(t // bm,),
        in_specs=[
            pl.BlockSpec((bm, d), lambda i: (i, 0)),
            pl.BlockSpec((1, d), lambda i: (0, 0)),
            pl.BlockSpec((d, d_in), lambda i: (0, 0)),
        ],
        out_specs=pl.BlockSpec((bm, d_in), lambda i: (i, 0)),
        out_shape=jax.ShapeDtypeStruct((t, d_in), F32),
        compiler_params=pltpu.CompilerParams(
            dimension_semantics=("arbitrary",),
            vmem_limit_bytes=_vmem_limit(2 * (bm * d * 4 + d * d_in * 2 + bm * d_in * 4) + 16 * 2**20)),
        name="in_proj",
    )(x, g, w)


def _mixer_body(z_ref, zp_ref, zn_ref, x_ref, pw_ref, ps_ref, ng_ref, sw_ref, sb_ref, wo_ref,
                o_ref, ext_ref, vn_ref, y_ref, *, seq):
    i = pl.program_id(0)
    bm = z_ref.shape[0]
    d_pool = ps_ref.shape[1]
    d_sgu = ng_ref.shape[1]
    pg = d_pool // len(POOL_WINDOWS)
    n_heads = sw_ref.shape[0]
    hd = d_sgu // n_heads

    ext_ref[0:HALO, :] = jnp.where(i > 0, zp_ref[...], 0.0)
    ext_ref[HALO:HALO + bm, :] = z_ref[:, 0:d_pool]
    ext_ref[HALO + bm:2 * HALO + bm, :] = jnp.where(i < pl.num_programs(0) - 1, zn_ref[...], 0.0)
    t = i * bm + lax.broadcasted_iota(jnp.int32, (bm, 1), 0)
    for g, w in enumerate(POOL_WINDOWS):
        cs = slice(g * pg, (g + 1) * pg)
        first = HALO - w // 2
        acc = ext_ref[first:first + bm, cs]
        for j in range(1, w):
            acc = acc + ext_ref[first + j:first + j + bm, cs]
        cnt = jnp.minimum(t + (w - w // 2), seq) - jnp.maximum(t - w // 2, 0)
        diff = acc * (1.0 / cnt.astype(F32)) - z_ref[:, cs]
        yp = jnp.dot(diff.astype(BF16), pw_ref[g], preferred_element_type=F32) * ps_ref[:, cs]
        y_ref[:, cs] = yp.astype(BF16)

    v = _gelu(z_ref[:, d_pool + d_sgu:d_pool + 2 * d_sgu])
    mu = jnp.mean(v, axis=-1, keepdims=True)
    vc = v - mu
    var = jnp.mean(vc * vc, axis=-1, keepdims=True)
    vn_ref[...] = (vc * lax.rsqrt(var + LN_EPS) * ng_ref[...]).astype(BF16)
    for c in range(bm // SGU_CHUNK):
        rs = slice(c * SGU_CHUNK, (c + 1) * SGU_CHUNK)
        for h in range(n_heads):
            hs = slice(h * hd, (h + 1) * hd)
            mixed = jnp.dot(sw_ref[h], vn_ref[rs, hs], preferred_element_type=F32) + sb_ref[:, hs]
            u = _gelu(z_ref[rs, d_pool + h * hd:d_pool + (h + 1) * hd])
            y_ref[rs, d_pool + h * hd:d_pool + (h + 1) * hd] = (u * mixed).astype(BF16)

    o_ref[...] = x_ref[...] + jnp.dot(y_ref[...], wo_ref[...], preferred_element_type=F32)


def _mixer(z, x, pool_w, pool_scale, norm_g, sgu_w, sgu_bias, w_out, *, bm):
    t, d = x.shape
    d_in = z.shape[1]
    d_pool = pool_scale.shape[1]
    d_sgu = norm_g.shape[1]
    d_mix = d_pool + d_sgu
    nh = t // HALO
    const2 = lambda i: (0, 0)
    const3 = lambda i: (0, 0, 0)
    return pl.pallas_call(
        functools.partial(_mixer_body, seq=t),
        grid=(t // bm,),
        in_specs=[
            pl.BlockSpec((bm, d_in), lambda i: (i, 0)),
            pl.BlockSpec((HALO, d_pool), lambda i: (jnp.maximum(i * (bm // HALO) - 1, 0), 0)),
            pl.BlockSpec((HALO, d_pool), lambda i: (jnp.minimum((i + 1) * (bm // HALO), nh - 1), 0)),
            pl.BlockSpec((bm, d), lambda i: (i, 0)),
            pl.BlockSpec(pool_w.shape, const3),
            pl.BlockSpec((1, d_pool), const2),
            pl.BlockSpec((1, d_sgu), const2),
            pl.BlockSpec(sgu_w.shape, const3),
            pl.BlockSpec(sgu_bias.shape, const2),
            pl.BlockSpec((d_mix, d), const2),
        ],
        out_specs=pl.BlockSpec((bm, d), lambda i: (i, 0)),
        out_shape=jax.ShapeDtypeStruct((t, d), F32),
        scratch_shapes=[
            pltpu.VMEM((bm + 2 * HALO, d_pool), F32),
            pltpu.VMEM((bm, d_sgu), BF16),
            pltpu.VMEM((bm, d_mix), BF16),
        ],
        compiler_params=pltpu.CompilerParams(
            dimension_semantics=("arbitrary",),
            vmem_limit_bytes=_vmem_limit(2 * (bm * d_in * 4 + 2 * bm * d * 4 + d_mix * d * 2) + 20 * 2**20)),
        name="mixer",
    )(z, z, z, x, pool_w, pool_scale, norm_g, sgu_w, sgu_bias, w_out)


N_EXPERT_MATS = 3


def _dense_ffn_body(x_ref, g_ref, wg_ref, wu_ref, wd_ref, *rest, half, rounding):
    if rounding:
        slabs_in, rest = rest[:N_EXPERT_MATS], rest[N_EXPERT_MATS:]
        o_ref, slabs_out, h_ref = rest[0], rest[1:1 + N_EXPERT_MATS], rest[-1]
        for src, dst in zip(slabs_in, slabs_out):
            dst[...] = src[...].astype(BF16)
    else:
        o_ref, h_ref = rest

    @pl.when(pl.program_id(1) == 0)
    def _():
        x = x_ref[...]
        h_ref[...] = _rmsnorm(x, g_ref[...]).astype(BF16)
        o_ref[...] = x

    for j in range(h_ref.shape[0] // half):
        rows = pl.ds(j * half, half)
        h = h_ref[rows, :]
        gate = jnp.dot(h, wg_ref[...], preferred_element_type=F32)
        up = jnp.dot(h, wu_ref[...], preferred_element_type=F32)
        act = (_silu(gate) * up).astype(BF16)
        o_ref[rows, :] += jnp.dot(act, wd_ref[...], preferred_element_type=F32)


def _expert_slab_split(n_row_blocks, n_experts, d):
    per = n_row_blocks // n_experts
    if per == 0 or n_row_blocks % n_experts or d % per or (d // per) % V7X_LANES:
        return 0
    return per


def _dense_ffn(x, g, w_gate, w_up, w_down, expert_mats=None, *, layer=0, bm, half, tf):
    t, d = x.shape
    d_ff = w_gate.shape[1]
    n_m = t // bm
    in_specs = [
        pl.BlockSpec((bm, d), lambda i, f: (i, 0)),
        pl.BlockSpec((1, d), lambda i, f: (0, 0)),
        pl.BlockSpec((d, tf), lambda i, f: (0, f)),
        pl.BlockSpec((d, tf), lambda i, f: (0, f)),
        pl.BlockSpec((tf, d), lambda i, f: (f, 0)),
    ]
    args = [x, g, w_gate, w_up, w_down]
    out_specs = [pl.BlockSpec((bm, d), lambda i, f: (i, 0))]
    out_shape = [jax.ShapeDtypeStruct((t, d), F32)]
    nbytes = 2 * (2 * bm * d * 4 + 3 * d * tf * 2) + bm * d * 2 + 3 * half * tf * 4
    rounding = expert_mats is not None
    if rounding:
        n_experts = expert_mats[0].shape[1]
        per = _expert_slab_split(n_m, n_experts, d)
        assert per, "the dense grid cannot walk the expert matrices once"
        rb = d // per
        up_like = pl.BlockSpec((None, None, rb, tf), lambda i, f: (layer, i // per, i % per, f))
        down_like = pl.BlockSpec((None, None, tf, rb), lambda i, f: (layer, i // per, f, i % per))
        in_specs += [up_like, up_like, down_like]
        args += list(expert_mats)
        out_specs += [pl.BlockSpec((None, rb, tf), lambda i, f: (i // per, i % per, f))] * 2
        out_specs += [pl.BlockSpec((None, tf, rb), lambda i, f: (i // per, f, i % per))]
        out_shape += [jax.ShapeDtypeStruct((n_experts, d, d_ff), BF16)] * 2
        out_shape += [jax.ShapeDtypeStruct((n_experts, d_ff, d), BF16)]
        nbytes += 2 * N_EXPERT_MATS * rb * tf * (4 + 2)
    res = pl.pallas_call(
        functools.partial(_dense_ffn_body, half=half, rounding=rounding),
        grid=(n_m, d_ff // tf),
        in_specs=in_specs,
        out_specs=out_specs,
        out_shape=out_shape,
        scratch_shapes=[pltpu.VMEM((bm, d), BF16)],
        compiler_params=pltpu.CompilerParams(
            dimension_semantics=("arbitrary", "arbitrary"), vmem_limit_bytes=_vmem_limit(nbytes + 8 * 2**20)),
        name="dense_ffn",
    )(*args)
    return (res[0], tuple(res[1:])) if rounding else res[0]


def _router_body(x_ref, g_ref, rw_ref, hp_ref, ri_ref, rg_ref, cnt_ref, run_ref, *, n_experts):
    i = pl.program_id(0)
    bm, d = x_ref.shape

    @pl.when(i == 0)
    def _():
        run_ref[...] = jnp.zeros_like(run_ref)

    h = _rmsnorm(x_ref[...], g_ref[...])

    h_hi = h.astype(BF16)
    bits = pltpu.bitcast(h_hi.astype(F32), jnp.uint32)
    hp_ref[...] = (bits[:, 0:d // 2] >> 16) | (bits[:, d // 2:d] & jnp.uint32(0xFFFF0000))

    h_lo = (h - h_hi.astype(F32)).astype(BF16)
    rw = rw_ref[...]
    rw_hi = rw.astype(BF16)
    rw_lo = (rw - rw_hi.astype(F32)).astype(BF16)
    logits = (jnp.dot(h_hi, rw_hi, preferred_element_type=F32)
              + (jnp.dot(h_lo, rw_hi, preferred_element_type=F32) + jnp.dot(h_hi, rw_lo, preferred_element_type=F32)))
    lane = lax.broadcasted_iota(jnp.int32, (bm, V7X_LANES), 1)
    neg = jnp.float32(-jnp.inf)
    lg = jnp.where(lane < n_experts, logits, neg)
    m1 = jnp.max(lg, axis=-1, keepdims=True)
    e1 = jnp.min(jnp.where(lg == m1, lane, V7X_LANES), axis=-1, keepdims=True)
    lg2 = jnp.where(lane == e1, neg, lg)
    m2 = jnp.max(lg2, axis=-1, keepdims=True)
    e2 = jnp.min(jnp.where(lg2 == m2, lane, V7X_LANES), axis=-1, keepdims=True)
    ex = jnp.exp(m2 - m1)
    g1 = 1.0 / (1.0 + ex)
    g2 = ex / (1.0 + ex)

    sel = jnp.where((lane == e1) | (lane == e2), 1.0, 0.0)
    r = lax.broadcasted_iota(jnp.int32, (bm, bm), 0)
    c = lax.broadcasted_iota(jnp.int32, (bm, bm), 1)
    tri = jnp.where(c < r, 1.0, 0.0).astype(BF16)
    prefix = jnp.dot(tri, sel.astype(BF16), preferred_element_type=F32) + run_ref[0:1, :]
    rank1 = jnp.sum(jnp.where(lane == e1, prefix, 0.0), axis=-1, keepdims=True).astype(jnp.int32)
    rank2 = jnp.sum(jnp.where(lane == e2, prefix, 0.0), axis=-1, keepdims=True).astype(jnp.int32)
    run_ref[0:1, :] = run_ref[0:1, :] + jnp.sum(sel, axis=0, keepdims=True)

    ri_ref[...] = jnp.where(lane == 0, e1, jnp.where(lane == 1, e2,
                            jnp.where(lane == 2, rank1, jnp.where(lane == 3, rank2, 0))))
    rg_ref[...] = jnp.where(lane == 0, g1, jnp.where(lane == 1, g2, 0.0))
    cnt_ref[...] = run_ref[...].astype(jnp.int32)


def _router(x, g, router_w_padded, *, bm, n_experts):
    t, d = x.shape
    return pl.pallas_call(
        functools.partial(_router_body, n_experts=n_experts),
        grid=(t // bm,),
        in_specs=[
            pl.BlockSpec((bm, d), lambda i: (i, 0)),
            pl.BlockSpec((1, d), lambda i: (0, 0)),
            pl.BlockSpec((d, V7X_LANES), lambda i: (0, 0)),
        ],
        out_specs=[
            pl.BlockSpec((bm, d // 2), lambda i: (i, 0)),
            pl.BlockSpec((bm, V7X_LANES), lambda i: (i, 0)),
            pl.BlockSpec((bm, V7X_LANES), lambda i: (i, 0)),
            pl.BlockSpec((V7X_SUBLANES, V7X_LANES), lambda i: (0, 0)),
        ],
        out_shape=[
            jax.ShapeDtypeStruct((t, d // 2), jnp.uint32),
            jax.ShapeDtypeStruct((t, V7X_LANES), jnp.int32),
            jax.ShapeDtypeStruct((t, V7X_LANES), F32),
            jax.ShapeDtypeStruct((V7X_SUBLANES, V7X_LANES), jnp.int32),
        ],
        scratch_shapes=[pltpu.VMEM((V7X_SUBLANES, V7X_LANES), F32)],
        compiler_params=pltpu.CompilerParams(dimension_semantics=("arbitrary",)),
        name="router",
    )(x, g, router_w_padded)


def _row_copy(src_hbm, dst_ref, sem, src_row, dst_row):
    return pltpu.make_async_copy(src_hbm.at[pl.ds(src_row, 1)], dst_ref.at[pl.ds(dst_row, 1)], sem)


def _expert_body(exp_ref, nv_ref, tok_ref, tokn_ref, hp_hbm, wg_ref, wu_ref, wd_ref, ys_hbm,
                 acc_ref, xw_ref, xs_ref, gsem, wsem, *, sub, half, rpf, n_gather):
    s = pl.program_id(0)
    f = pl.program_id(1)
    ns = pl.num_programs(0)
    nf = pl.num_programs(1)
    sb, d = acc_ref.shape
    nsub = sb // sub
    nvalid = nv_ref[s]
    slot = s % 2

    def writeback(blk):
        return pltpu.make_async_copy(acc_ref, ys_hbm.at[pl.ds(blk * sb, sb)], wsem)

    def gather(tokens_ref, buf, r):
        return _row_copy(hp_hbm, xw_ref.at[buf], gsem.at[buf], tokens_ref[jnp.minimum(r, sb - 1)], r)

    def drain(buf):
        def body(r, carry):
            _row_copy(hp_hbm, xw_ref.at[buf], gsem.at[buf], 0, r).wait()
            return carry
        lax.fori_loop(0, n_gather, body, 0, unroll=8)

    @pl.when(f == 0)
    def _():
        @pl.when(s == 0)
        def _():
            def issue(r, carry):
                gather(tok_ref, slot, r).start()
                return carry
            lax.fori_loop(0, n_gather, issue, 0, unroll=8)

        @pl.when(s > 0)
        def _():
            writeback(s - 1).wait()
        acc_ref[...] = jnp.zeros_like(acc_ref)

        drain(slot)
        for j in range(nsub):
            rows = pl.ds(j * sub, sub)

            @pl.when(j * sub < nvalid)
            def _():
                w = xw_ref[slot, rows, :]
                xs_ref[rows, 0:d // 2] = pltpu.bitcast(w << 16, F32).astype(BF16)
                xs_ref[rows, d // 2:d] = pltpu.bitcast(w & jnp.uint32(0xFFFF0000), F32).astype(BF16)

    def ffn(rows):
        x = xs_ref[rows, :]
        gate = jnp.dot(x, wg_ref[...], preferred_element_type=F32)
        up = jnp.dot(x, wu_ref[...], preferred_element_type=F32)
        act = (_silu(gate) * up).astype(BF16)
        acc_ref[rows, :] += jnp.dot(act, wd_ref[...], preferred_element_type=F32)

    full = nvalid > (nsub - 1) * sub
    n_halves = sb // half

    @pl.when(full)
    def _():
        for j in range(n_halves):
            if j == n_halves - 1:
                for k in range(rpf):
                    gather(tokn_ref, 1 - slot, f * rpf + k).start(priority=1)
            ffn(pl.ds(j * half, half))

    @pl.when(jnp.logical_not(full))
    def _():
        def issue(k, carry):
            gather(tokn_ref, 1 - slot, f * rpf + k).start(priority=1)
            return carry
        lax.fori_loop(0, rpf, issue, 0, unroll=8)

    @pl.when(jnp.logical_and(nvalid > 0, jnp.logical_not(full)))
    def _():
        for j in range(nsub - 1):
            @pl.when(j * sub < nvalid)
            def _():
                ffn(pl.ds(j * sub, sub))

    @pl.when(f == nf - 1)
    def _():
        writeback(s).start()

        @pl.when(s == ns - 1)
        def _():
            writeback(s).wait()
            drain(1 - slot)


def _experts(sb_expert, sb_nvalid, slot_tok, h_packed, w_gate, w_up, w_down, *, sb, sub, half, tf):
    n_sb = sb_expert.shape[0]
    _, d, d_ff = w_gate.shape
    nf = d_ff // tf
    rpf = pl.cdiv(sb, nf)
    n_gather = rpf * nf

    def f_idx(s, f, nv_ref):
        return jnp.where(nv_ref[s] > 0, f, nf - 1)

    grid_spec = pltpu.PrefetchScalarGridSpec(
        num_scalar_prefetch=2,
        grid=(n_sb, nf),
        in_specs=[
            pl.BlockSpec((sb,), lambda s, f, e, nv: (s,), memory_space=pltpu.SMEM),
            pl.BlockSpec((sb,), lambda s, f, e, nv: (jnp.minimum(s + 1, n_sb - 1),), memory_space=pltpu.SMEM),
            pl.BlockSpec(memory_space=pl.ANY),
            pl.BlockSpec((None, d, tf), lambda s, f, e, nv: (e[s], 0, f_idx(s, f, nv))),
            pl.BlockSpec((None, d, tf), lambda s, f, e, nv: (e[s], 0, f_idx(s, f, nv))),
            pl.BlockSpec((None, tf, d), lambda s, f, e, nv: (e[s], f_idx(s, f, nv), 0)),
        ],
        out_specs=pl.BlockSpec(memory_space=pl.ANY),
        scratch_shapes=[
            pltpu.VMEM((sb, d), F32),
            pltpu.VMEM((2, pl.cdiv(n_gather, V7X_SUBLANES) * V7X_SUBLANES, d // 2), jnp.uint32),
            pltpu.VMEM((sb, d), BF16),
            pltpu.SemaphoreType.DMA((2,)),
            pltpu.SemaphoreType.DMA,
        ],
    )
    return pl.pallas_call(
        functools.partial(_expert_body, sub=sub, half=half, rpf=rpf, n_gather=n_gather),
        grid_spec=grid_spec,
        out_shape=jax.ShapeDtypeStruct((n_sb * sb, d), F32),
        compiler_params=pltpu.CompilerParams(
            dimension_semantics=("arbitrary", "arbitrary"),
            vmem_limit_bytes=_vmem_limit(2 * 3 * d * tf * 2 + sb * d * (4 + 2)
                                         + 2 * n_gather * d * 2 + 4 * half * tf * 4 + 8 * 2**20)),
        name="experts",
    )(sb_expert, sb_nvalid, slot_tok, slot_tok, h_packed, w_gate, w_up, w_down)


def _combine_body(dest_ref, destn_ref, rg_ref, x_ref, ys_hbm, g_ref, *rest, proj):
    if proj:
        w_ref, xo_ref, z_ref, y_ref, sem = rest
    else:
        o_ref, y_ref, sem = rest
    i = pl.program_id(0)
    bt = x_ref.shape[0]
    slot = i % 2

    def fetch(d_ref, buf, r, k):
        return _row_copy(ys_hbm, y_ref.at[buf, k], sem.at[buf], d_ref[TOP_K * r + k], r)

    def drain(buf):
        def body(r, carry):
            for k in range(TOP_K):
                _row_copy(ys_hbm, y_ref.at[buf, k], sem.at[buf], 0, r).wait()
            return carry
        lax.fori_loop(0, bt, body, 0, unroll=8)

    @pl.when(i == 0)
    def _():
        def issue(r, carry):
            for k in range(TOP_K):
                fetch(dest_ref, slot, r, k).start()
            return carry
        lax.fori_loop(0, bt, issue, 0, unroll=8)

    drain(slot)
    out = x_ref[...] + (rg_ref[:, 0:1] * y_ref[slot, 0] + rg_ref[:, 1:2] * y_ref[slot, 1])
    if proj:
        xo_ref[...] = out
    for r in range(bt):
        for k in range(TOP_K):
            fetch(destn_ref, 1 - slot, r, k).start(priority=k)
    h = _rmsnorm(out, g_ref[...])
    if proj:
        z_ref[...] = jnp.dot(h.astype(BF16), w_ref[...], preferred_element_type=F32)
    else:
        o_ref[...] = h

    @pl.when(i == pl.num_programs(0) - 1)
    def _():
        drain(1 - slot)


def _combine(dest, rg, x, ys, g, w=None, *, bt):
    t, d = x.shape
    n = t // bt
    proj = w is not None
    in_specs = [
        pl.BlockSpec((TOP_K * bt,), lambda i: (i,), memory_space=pltpu.SMEM),
        pl.BlockSpec((TOP_K * bt,), lambda i: (jnp.minimum(i + 1, n - 1),), memory_space=pltpu.SMEM),
        pl.BlockSpec((bt, V7X_LANES), lambda i: (i, 0)),
        pl.BlockSpec((bt, d), lambda i: (i, 0)),
        pl.BlockSpec(memory_space=pl.ANY),
        pl.BlockSpec((1, d), lambda i: (0, 0)),
    ]
    args = [dest, dest, rg, x, ys, g]
    out_specs = [pl.BlockSpec((bt, d), lambda i: (i, 0))]
    out_shape = [jax.ShapeDtypeStruct((t, d), F32)]
    nbytes = 2 * 2 * bt * d * 4 + 2 * TOP_K * bt * d * 4
    if proj:
        d_in = w.shape[1]
        in_specs.append(pl.BlockSpec((d, d_in), lambda i: (0, 0)))
        args.append(w)
        out_specs.append(pl.BlockSpec((bt, d_in), lambda i: (i, 0)))
        out_shape.append(jax.ShapeDtypeStruct((t, d_in), F32))
        nbytes += 2 * (bt * d_in * 4 + d * d_in * 2)
    res = pl.pallas_call(
        functools.partial(_combine_body, proj=proj),
        grid=(n,),
        in_specs=in_specs,
        out_specs=out_specs,
        out_shape=out_shape,
        scratch_shapes=[pltpu.VMEM((2, TOP_K, bt, d), F32), pltpu.SemaphoreType.DMA((2,))],
        compiler_params=pltpu.CompilerParams(
            dimension_semantics=("arbitrary",), vmem_limit_bytes=_vmem_limit(nbytes + 8 * 2**20)),
        name="combine",
    )(*args)
    return tuple(res) if proj else res[0]


def _routing_tables(ri, cnt, *, n_experts, sb, n_sb):
    t = ri.shape[0]
    experts = ri[:, 0:TOP_K]
    ranks = ri[:, TOP_K:2 * TOP_K]
    counts = cnt[0, 0:n_experts]
    nsb_e = (counts + sb - 1) // sb
    sb_end = jnp.cumsum(nsb_e)
    sb_start = sb_end - nsb_e
    dest = (sb_start * sb)[experts] + ranks
    tok = jnp.broadcast_to(jnp.arange(t, dtype=jnp.int32)[:, None], (t, TOP_K))
    slot_tok = jnp.zeros((n_sb * sb,), jnp.int32).at[dest.reshape(-1)].set(tok.reshape(-1))
    ids = jnp.arange(n_sb, dtype=jnp.int32)
    used = ids < sb_end[-1]
    owner_of = jnp.where(used, ids, jnp.maximum(sb_end[-1] - 1, 0))
    sb_expert = jnp.minimum(jnp.sum(sb_end[None, :] <= owner_of[:, None], axis=1), n_experts - 1).astype(jnp.int32)
    nvalid = jnp.clip(counts[sb_expert] - (ids - sb_start[sb_expert]) * sb, 0, sb)
    sb_nvalid = jnp.where(used, nvalid, 0).astype(jnp.int32)
    return sb_expert, sb_nvalid, slot_tok, dest.reshape(-1).astype(jnp.int32)


def _block(n, pref, mult):
    if n <= pref:
        return n
    b = (pref // mult) * mult
    while n % b:
        b -= mult
    return b


def _forward(x, norm_mix_g, w_in, pool_w, pool_scale, sgu_norm_g, sgu_w, sgu_b, w_out, norm_ffn_g,
             ffn_w_gate, ffn_w_up, ffn_w_down, router_w, moe_w_gate, moe_w_up, moe_w_down,
             final_norm_g, *, bm_mix=512, bm_ffn=512, tf_dense=256, tf_moe=512, sb=1024, sub=256, half=512,
             bt=256):
    b, s, d = x.shape
    assert b == 1, "pooling windows and gating chunks are laid out for a single sequence"
    t = b * s
    depth = w_in.shape[0]
    n_experts = router_w.shape[-1]
    n_heads, chunk = sgu_b.shape[1:]
    d_sgu = sgu_norm_g.shape[1]
    assert chunk == SGU_CHUNK and depth % 2 == 0 and n_experts <= V7X_LANES
    bm_mix = _block(t, bm_mix, SGU_CHUNK)
    bm_ffn = _block(t, bm_ffn, V7X_SUBLANES)
    bt = _block(t, bt, V7X_SUBLANES)
    sb = min(sb, t)
    sub = min(sub, sb)
    half = min(half, sb)
    bm_dense = _block(t, sb, half)
    tf_dense = _block(ffn_w_gate.shape[-1], tf_dense, V7X_LANES)
    tf_moe = _block(moe_w_gate.shape[-1], tf_moe, V7X_LANES)
    n_sb = (t * TOP_K + n_experts * (sb - 1)) // sb

    x = x.reshape(t, d)
    row = lambda a: a.reshape(1, -1)
    z = None
    for l in range(depth):
        if z is None:
            z = _in_proj(x, row(norm_mix_g[l]), w_in[l].astype(BF16), bm=bm_mix)
        bias = jnp.broadcast_to(sgu_b[l].T[:, :, None], (chunk, n_heads, d_sgu // n_heads)).reshape(chunk, d_sgu)
        x = _mixer(z, x, pool_w[l].astype(BF16), row(pool_scale[l]), row(sgu_norm_g[l]),
                   sgu_w[l].astype(BF16), bias, w_out[l].astype(BF16), bm=bm_mix)
        z = None
        i = l // 2
        if l % 2 == 0:
            dense_w = (ffn_w_gate[i].astype(BF16), ffn_w_up[i].astype(BF16), ffn_w_down[i].astype(BF16))
            if _expert_slab_split(t // bm_dense, n_experts, d):
                x, expert_w = _dense_ffn(x, row(norm_ffn_g[l]), *dense_w, (moe_w_gate, moe_w_up, moe_w_down),
                                         layer=i, bm=bm_dense, half=half, tf=tf_dense)
            else:
                x = _dense_ffn(x, row(norm_ffn_g[l]), *dense_w, bm=bm_dense, half=half, tf=tf_dense)
                expert_w = (moe_w_gate[i].astype(BF16), moe_w_up[i].astype(BF16), moe_w_down[i].astype(BF16))
        else:
            rw = jnp.pad(router_w[i], ((0, 0), (0, V7X_LANES - n_experts)))
            h_packed, ri, rg, cnt = _router(x, row(norm_ffn_g[l]), rw, bm=bm_ffn, n_experts=n_experts)
            sb_expert, sb_nvalid, slot_tok, dest = _routing_tables(ri, cnt, n_experts=n_experts, sb=sb, n_sb=n_sb)
            ys = _experts(sb_expert, sb_nvalid, slot_tok, h_packed, *expert_w, sb=sb, sub=sub, half=half, tf=tf_moe)
            if l == depth - 1:
                x = _combine(dest, rg, x, ys, row(final_norm_g), bt=bt)
            else:
                x, z = _combine(dest, rg, x, ys, row(norm_mix_g[l + 1]), w_in[l + 1].astype(BF16), bt=bt)
    return x.reshape(b, s, d)


def kernel(x, norm_mix_g, w_in, pool_w, pool_scale, sgu_norm_g, sgu_w, sgu_b, w_out, norm_ffn_g,
           ffn_w_gate, ffn_w_up, ffn_w_down, router_w, moe_w_gate, moe_w_up, moe_w_down, final_norm_g):
    return _forward(x, norm_mix_g, w_in, pool_w, pool_scale, sgu_norm_g, sgu_w, sgu_b, w_out, norm_ffn_g,
                    ffn_w_gate, ffn_w_up, ffn_w_down, router_w, moe_w_gate, moe_w_up, moe_w_down,
                    final_norm_g)
```

```python
import functools

import jax
import jax.numpy as jnp
from jax import lax
from jax.experimental import pallas as pl
from jax.experimental.pallas import tpu as pltpu

RMS_EPS = 1e-6
LN_EPS = 1e-5
POOL_WINDOWS = (2, 4, 8, 16)
SGU_CHUNK = 128
TOP_K = 2

V7X_LANES = 128
V7X_SUBLANES = 8
V7X_VMEM_BYTES = 64 * 1024 * 1024
HALO = max(POOL_WINDOWS) // 2
assert HALO == V7X_SUBLANES

BF16 = jnp.bfloat16
F32 = jnp.float32


def _vmem_limit(nbytes):
    return int(min(nbytes, V7X_VMEM_BYTES - 6 * 1024 * 1024))


def _rmsnorm(x, g):
    return x * lax.rsqrt(jnp.mean(x * x, axis=-1, keepdims=True) + RMS_EPS) * g


def _gelu(x):
    return 0.5 * x * (1.0 + lax.erf(x * (2.0 ** -0.5)))


def _silu(x):
    return x / (1.0 + jnp.exp(-x))


def _in_proj_body(x_ref, g_ref, w_ref, z_ref):
    h = _rmsnorm(x_ref[...], g_ref[...]).astype(BF16)
    z_ref[...] = jnp.dot(h, w_ref[...], preferred_element_type=F32)


def _in_proj(x, g, w, *, layer, bm):
    t, d = x.shape
    d_in = w.shape[2]
    return pl.pallas_call(
        _in_proj_body,
        grid=(t // bm,),
        in_specs=[
            pl.BlockSpec((bm, d), lambda i: (i, 0)),
            pl.BlockSpec((1, d), lambda i: (0, 0)),
            pl.BlockSpec((None, d, d_in), lambda i: (layer, 0, 0)),
        ],
        out_specs=pl.BlockSpec((bm, d_in), lambda i: (i, 0)),
        out_shape=jax.ShapeDtypeStruct((t, d_in), F32),
        compiler_params=pltpu.CompilerParams(
            dimension_semantics=("arbitrary",),
            vmem_limit_bytes=_vmem_limit(2 * (bm * d * 4 + d * d_in * 2 + bm * d_in * 4) + 16 * 2**20)),
        name="in_proj",
    )(x, g, w)


def _mixer_body(z_ref, zp_ref, zn_ref, x_ref, pw_ref, ps_ref, ng_ref, sw_ref, sb_ref, wo_ref,
                o_ref, ext_ref, vn_ref, y_ref, *, seq):
    i = pl.program_id(0)
    bm = z_ref.shape[0]
    d_pool = ps_ref.shape[1]
    d_sgu = ng_ref.shape[1]
    pg = d_pool // len(POOL_WINDOWS)
    n_heads = sw_ref.shape[0]
    hd = d_sgu // n_heads

    ext_ref[0:HALO, :] = jnp.where(i > 0, zp_ref[...], 0.0)
    ext_ref[HALO:HALO + bm, :] = z_ref[:, 0:d_pool]
    ext_ref[HALO + bm:2 * HALO + bm, :] = jnp.where(i < pl.num_programs(0) - 1, zn_ref[...], 0.0)
    t = i * bm + lax.broadcasted_iota(jnp.int32, (bm, 1), 0)
    for g, w in enumerate(POOL_WINDOWS):
        cs = slice(g * pg, (g + 1) * pg)
        first = HALO - w // 2
        acc = ext_ref[first:first + bm, cs]
        for j in range(1, w):
            acc = acc + ext_ref[first + j:first + j + bm, cs]
        cnt = jnp.minimum(t + (w - w // 2), seq) - jnp.maximum(t - w // 2, 0)
        diff = acc * (1.0 / cnt.astype(F32)) - z_ref[:, cs]
        yp = jnp.dot(diff.astype(BF16), pw_ref[g], preferred_element_type=F32) * ps_ref[:, cs]
        y_ref[:, cs] = yp.astype(BF16)

    v = _gelu(z_ref[:, d_pool + d_sgu:d_pool + 2 * d_sgu])
    mu = jnp.mean(v, axis=-1, keepdims=True)
    vc = v - mu
    var = jnp.mean(vc * vc, axis=-1, keepdims=True)
    vn_ref[...] = (vc * lax.rsqrt(var + LN_EPS) * ng_ref[...]).astype(BF16)
    for c in range(bm // SGU_CHUNK):
        rs = slice(c * SGU_CHUNK, (c + 1) * SGU_CHUNK)
        for h in range(n_heads):
            hs = slice(h * hd, (h + 1) * hd)
            mixed = jnp.dot(sw_ref[h], vn_ref[rs, hs], preferred_element_type=F32) + sb_ref[:, hs]
            u = _gelu(z_ref[rs, d_pool + h * hd:d_pool + (h + 1) * hd])
            y_ref[rs, d_pool + h * hd:d_pool + (h + 1) * hd] = (u * mixed).astype(BF16)

    o_ref[...] = x_ref[...] + jnp.dot(y_ref[...], wo_ref[...], preferred_element_type=F32)


def _mixer(z, x, pool_w, pool_scale, norm_g, sgu_w, sgu_bias, w_out, *, layer, bm):
    t, d = x.shape
    d_in = z.shape[1]
    d_pool = pool_scale.shape[1]
    d_sgu = norm_g.shape[1]
    d_mix = d_pool + d_sgu
    nh = t // HALO
    const2 = lambda i: (0, 0)
    return pl.pallas_call(
        functools.partial(_mixer_body, seq=t),
        grid=(t // bm,),
        in_specs=[
            pl.BlockSpec((bm, d_in), lambda i: (i, 0)),
            pl.BlockSpec((HALO, d_pool), lambda i: (jnp.maximum(i * (bm // HALO) - 1, 0), 0)),
            pl.BlockSpec((HALO, d_pool), lambda i: (jnp.minimum((i + 1) * (bm // HALO), nh - 1), 0)),
            pl.BlockSpec((bm, d), lambda i: (i, 0)),
            pl.BlockSpec((None,) + pool_w.shape[1:], lambda i: (layer, 0, 0, 0)),
            pl.BlockSpec((1, d_pool), const2),
            pl.BlockSpec((1, d_sgu), const2),
            pl.BlockSpec((None,) + sgu_w.shape[1:], lambda i: (layer, 0, 0, 0)),
            pl.BlockSpec(sgu_bias.shape, const2),
            pl.BlockSpec((None, d_mix, d), lambda i: (layer, 0, 0)),
        ],
        out_specs=pl.BlockSpec((bm, d), lambda i: (i, 0)),
        out_shape=jax.ShapeDtypeStruct((t, d), F32),
        scratch_shapes=[
            pltpu.VMEM((bm + 2 * HALO, d_pool), F32),
            pltpu.VMEM((bm, d_sgu), BF16),
            pltpu.VMEM((bm, d_mix), BF16),
        ],
        compiler_params=pltpu.CompilerParams(
            dimension_semantics=("arbitrary",),
            vmem_limit_bytes=_vmem_limit(2 * (bm * d_in * 4 + 2 * bm * d * 4 + d_mix * d * 2) + 20 * 2**20)),
        name="mixer",
    )(z, z, z, x, pool_w, pool_scale, norm_g, sgu_w, sgu_bias, w_out)


def _dense_ffn_body(x_ref, g_ref, wg_ref, wu_ref, wd_ref, o_ref, h_ref, *, half):
    @pl.when(pl.program_id(1) == 0)
    def _():
        x = x_ref[...]
        h_ref[...] = _rmsnorm(x, g_ref[...]).astype(BF16)
        o_ref[...] = x

    for j in range(h_ref.shape[0] // half):
        rows = pl.ds(j * half, half)
        h = h_ref[rows, :]
        gate = jnp.dot(h, wg_ref[...], preferred_element_type=F32)
        up = jnp.dot(h, wu_ref[...], preferred_element_type=F32)
        act = (_silu(gate) * up).astype(BF16)
        o_ref[rows, :] += jnp.dot(act, wd_ref[...], preferred_element_type=F32)


def _dense_ffn(x, g, w_gate, w_up, w_down, *, layer, bm, half, tf):
    t, d = x.shape
    d_ff = w_gate.shape[2]
    return pl.pallas_call(
        functools.partial(_dense_ffn_body, half=half),
        grid=(t // bm, d_ff // tf),
        in_specs=[
            pl.BlockSpec((bm, d), lambda i, f: (i, 0)),
            pl.BlockSpec((1, d), lambda i, f: (0, 0)),
            pl.BlockSpec((None, d, tf), lambda i, f: (layer, 0, f)),
            pl.BlockSpec((None, d, tf), lambda i, f: (layer, 0, f)),
            pl.BlockSpec((None, tf, d), lambda i, f: (layer, f, 0)),
        ],
        out_specs=pl.BlockSpec((bm, d), lambda i, f: (i, 0)),
        out_shape=jax.ShapeDtypeStruct((t, d), F32),
        scratch_shapes=[pltpu.VMEM((bm, d), BF16)],
        compiler_params=pltpu.CompilerParams(
            dimension_semantics=("arbitrary", "arbitrary"),
            vmem_limit_bytes=_vmem_limit(2 * (2 * bm * d * 4 + 3 * d * tf * 2) + bm * d * 2
                                         + 3 * half * tf * 4 + 8 * 2**20)),
        name="dense_ffn",
    )(x, g, w_gate, w_up, w_down)


def _router_body(x_ref, g_ref, rw_ref, hp_ref, ri_ref, rg_ref, cnt_ref, run_ref, *, n_experts):
    i = pl.program_id(0)
    bm, d = x_ref.shape

    @pl.when(i == 0)
    def _():
        run_ref[...] = jnp.zeros_like(run_ref)

    h = _rmsnorm(x_ref[...], g_ref[...])

    h_hi = h.astype(BF16)
    bits = pltpu.bitcast(h_hi.astype(F32), jnp.uint32)
    hp_ref[...] = (bits[:, 0:d // 2] >> 16) | (bits[:, d // 2:d] & jnp.uint32(0xFFFF0000))

    h_lo = (h - h_hi.astype(F32)).astype(BF16)
    rw = rw_ref[...]
    rw_hi = rw.astype(BF16)
    rw_lo = (rw - rw_hi.astype(F32)).astype(BF16)
    logits = (jnp.dot(h_hi, rw_hi, preferred_element_type=F32)
              + (jnp.dot(h_lo, rw_hi, preferred_element_type=F32) + jnp.dot(h_hi, rw_lo, preferred_element_type=F32)))
    lane = lax.broadcasted_iota(jnp.int32, (bm, V7X_LANES), 1)
    neg = jnp.float32(-jnp.inf)
    lg = jnp.where(lane < n_experts, logits, neg)
    m1 = jnp.max(lg, axis=-1, keepdims=True)
    e1 = jnp.min(jnp.where(lg == m1, lane, V7X_LANES), axis=-1, keepdims=True)
    lg2 = jnp.where(lane == e1, neg, lg)
    m2 = jnp.max(lg2, axis=-1, keepdims=True)
    e2 = jnp.min(jnp.where(lg2 == m2, lane, V7X_LANES), axis=-1, keepdims=True)
    ex = jnp.exp(m2 - m1)
    g1 = 1.0 / (1.0 + ex)
    g2 = ex / (1.0 + ex)

    sel = jnp.where((lane == e1) | (lane == e2), 1.0, 0.0)
    r = lax.broadcasted_iota(jnp.int32, (bm, bm), 0)
    c = lax.broadcasted_iota(jnp.int32, (bm, bm), 1)
    tri = jnp.where(c < r, 1.0, 0.0).astype(BF16)
    prefix = jnp.dot(tri, sel.astype(BF16), preferred_element_type=F32) + run_ref[0:1, :]
    rank1 = jnp.sum(jnp.where(lane == e1, prefix, 0.0), axis=-1, keepdims=True).astype(jnp.int32)
    rank2 = jnp.sum(jnp.where(lane == e2, prefix, 0.0), axis=-1, keepdims=True).astype(jnp.int32)
    run_ref[0:1, :] = run_ref[0:1, :] + jnp.sum(sel, axis=0, keepdims=True)

    ri_ref[...] = jnp.where(lane == 0, e1, jnp.where(lane == 1, e2,
                            jnp.where(lane == 2, rank1, jnp.where(lane == 3, rank2, 0))))
    rg_ref[...] = jnp.where(lane == 0, g1, jnp.where(lane == 1, g2, 0.0))
    cnt_ref[...] = run_ref[...].astype(jnp.int32)


def _router(x, g, router_w_padded, *, bm, n_experts):
    t, d = x.shape
    return pl.pallas_call(
        functools.partial(_router_body, n_experts=n_experts),
        grid=(t // bm,),
        in_specs=[
            pl.BlockSpec((bm, d), lambda i: (i, 0)),
            pl.BlockSpec((1, d), lambda i: (0, 0)),
            pl.BlockSpec((d, V7X_LANES), lambda i: (0, 0)),
        ],
        out_specs=[
            pl.BlockSpec((bm, d // 2), lambda i: (i, 0)),
            pl.BlockSpec((bm, V7X_LANES), lambda i: (i, 0)),
            pl.BlockSpec((bm, V7X_LANES), lambda i: (i, 0)),
            pl.BlockSpec((V7X_SUBLANES, V7X_LANES), lambda i: (0, 0)),
        ],
        out_shape=[
            jax.ShapeDtypeStruct((t, d // 2), jnp.uint32),
            jax.ShapeDtypeStruct((t, V7X_LANES), jnp.int32),
            jax.ShapeDtypeStruct((t, V7X_LANES), F32),
            jax.ShapeDtypeStruct((V7X_SUBLANES, V7X_LANES), jnp.int32),
        ],
        scratch_shapes=[pltpu.VMEM((V7X_SUBLANES, V7X_LANES), F32)],
        compiler_params=pltpu.CompilerParams(dimension_semantics=("arbitrary",)),
        name="router",
    )(x, g, router_w_padded)


def _row_copy(src_hbm, dst_ref, sem, src_row, dst_row):
    return pltpu.make_async_copy(src_hbm.at[pl.ds(src_row, 1)], dst_ref.at[pl.ds(dst_row, 1)], sem)


def _expert_body(exp_ref, nv_ref, tok_ref, tokn_ref, hp_hbm, wg_ref, wu_ref, wd_ref, ys_hbm,
                 acc_ref, xw_ref, xs_ref, wgb_ref, wub_ref, wdb_ref, gsem, wsem, *, sub, half, rpf, n_gather):
    s = pl.program_id(0)
    f = pl.program_id(1)
    ns = pl.num_programs(0)
    nf = pl.num_programs(1)
    sb, d = acc_ref.shape
    nsub = sb // sub
    nvalid = nv_ref[s]
    slot = s % 2

    def writeback(blk):
        return pltpu.make_async_copy(acc_ref, ys_hbm.at[pl.ds(blk * sb, sb)], wsem)

    def gather(tokens_ref, buf, r):
        return _row_copy(hp_hbm, xw_ref.at[buf], gsem.at[buf], tokens_ref[jnp.minimum(r, sb - 1)], r)

    def drain(buf):
        def body(r, carry):
            _row_copy(hp_hbm, xw_ref.at[buf], gsem.at[buf], 0, r).wait()
            return carry
        lax.fori_loop(0, n_gather, body, 0, unroll=8)

    @pl.when(f == 0)
    def _():
        @pl.when(s == 0)
        def _():
            def issue(r, carry):
                gather(tok_ref, slot, r).start()
                return carry
            lax.fori_loop(0, n_gather, issue, 0, unroll=8)

        @pl.when(s > 0)
        def _():
            writeback(s - 1).wait()
        acc_ref[...] = jnp.zeros_like(acc_ref)

        drain(slot)
        for j in range(nsub):
            rows = pl.ds(j * sub, sub)

            @pl.when(j * sub < nvalid)
            def _():
                w = xw_ref[slot, rows, :]
                xs_ref[rows, 0:d // 2] = pltpu.bitcast(w << 16, F32).astype(BF16)
                xs_ref[rows, d // 2:d] = pltpu.bitcast(w & jnp.uint32(0xFFFF0000), F32).astype(BF16)

    def cast_weights():
        wgb_ref[...] = wg_ref[...].astype(BF16)
        wub_ref[...] = wu_ref[...].astype(BF16)
        wdb_ref[...] = wd_ref[...].astype(BF16)

    def ffn(rows):
        x = xs_ref[rows, :]
        gate = jnp.dot(x, wgb_ref[...], preferred_element_type=F32)
        up = jnp.dot(x, wub_ref[...], preferred_element_type=F32)
        act = (_silu(gate) * up).astype(BF16)
        acc_ref[rows, :] += jnp.dot(act, wdb_ref[...], preferred_element_type=F32)

    full = nvalid > (nsub - 1) * sub
    n_halves = sb // half

    @pl.when(full)
    def _():
        cast_weights()
        for j in range(n_halves):
            if j == n_halves - 1:
                for k in range(rpf):
                    gather(tokn_ref, 1 - slot, f * rpf + k).start(priority=1)
            ffn(pl.ds(j * half, half))

    @pl.when(jnp.logical_and(nvalid > 0, jnp.logical_not(full)))
    def _():
        cast_weights()
        for j in range(nsub - 1):
            @pl.when(j * sub < nvalid)
            def _():
                if j == 0:
                    for k in range(rpf):
                        gather(tokn_ref, 1 - slot, f * rpf + k).start(priority=1)
                ffn(pl.ds(j * sub, sub))

    @pl.when(nvalid == 0)
    def _():
        def issue(k, carry):
            gather(tokn_ref, 1 - slot, f * rpf + k).start(priority=1)
            return carry
        lax.fori_loop(0, rpf, issue, 0, unroll=8)

    @pl.when(f == nf - 1)
    def _():
        writeback(s).start()

        @pl.when(s == ns - 1)
        def _():
            writeback(s).wait()
            drain(1 - slot)


def _experts(sb_expert, sb_nvalid, slot_tok, h_packed, w_gate, w_up, w_down, *, layer, sb, sub, half, tf):
    n_sb = sb_expert.shape[0]
    _, _, d, d_ff = w_gate.shape
    nf = d_ff // tf
    rpf = pl.cdiv(sb, nf)
    n_gather = rpf * nf

    def f_idx(s, f, nv_ref):
        return jnp.where(nv_ref[s] > 0, f, nf - 1)

    grid_spec = pltpu.PrefetchScalarGridSpec(
        num_scalar_prefetch=2,
        grid=(n_sb, nf),
        in_specs=[
            pl.BlockSpec((sb,), lambda s, f, e, nv: (s,), memory_space=pltpu.SMEM),
            pl.BlockSpec((sb,), lambda s, f, e, nv: (jnp.minimum(s + 1, n_sb - 1),), memory_space=pltpu.SMEM),
            pl.BlockSpec(memory_space=pl.ANY),
            pl.BlockSpec((None, None, d, tf), lambda s, f, e, nv: (layer, e[s], 0, f_idx(s, f, nv))),
            pl.BlockSpec((None, None, d, tf), lambda s, f, e, nv: (layer, e[s], 0, f_idx(s, f, nv))),
            pl.BlockSpec((None, None, tf, d), lambda s, f, e, nv: (layer, e[s], f_idx(s, f, nv), 0)),
        ],
        out_specs=pl.BlockSpec(memory_space=pl.ANY),
        scratch_shapes=[
            pltpu.VMEM((sb, d), F32),
            pltpu.VMEM((2, pl.cdiv(n_gather, V7X_SUBLANES) * V7X_SUBLANES, d // 2), jnp.uint32),
            pltpu.VMEM((sb, d), BF16),
            pltpu.VMEM((d, tf), BF16),
            pltpu.VMEM((d, tf), BF16),
            pltpu.VMEM((tf, d), BF16),
            pltpu.SemaphoreType.DMA((2,)),
            pltpu.SemaphoreType.DMA,
        ],
    )
    weight_bytes = 3 * d * tf * 4
    return pl.pallas_call(
        functools.partial(_expert_body, sub=sub, half=half, rpf=rpf, n_gather=n_gather),
        grid_spec=grid_spec,
        out_shape=jax.ShapeDtypeStruct((n_sb * sb, d), F32),
        compiler_params=pltpu.CompilerParams(
            dimension_semantics=("arbitrary", "arbitrary"),
            vmem_limit_bytes=_vmem_limit(2 * weight_bytes + weight_bytes // 2 + sb * d * (4 + 2)
                                         + 2 * n_gather * d * 2 + 3 * half * tf * 4 + 8 * 2**20)),
        name="experts",
    )(sb_expert, sb_nvalid, slot_tok, slot_tok, h_packed, w_gate, w_up, w_down)


def _combine_body(dest_ref, destn_ref, rg_ref, x_ref, ys_hbm, g_ref, *rest, proj):
    if proj:
        w_ref, xo_ref, z_ref, y_ref, sem = rest
    else:
        o_ref, y_ref, sem = rest
    i = pl.program_id(0)
    bt = x_ref.shape[0]
    slot = i % 2

    def fetch(d_ref, buf, r, k):
        return _row_copy(ys_hbm, y_ref.at[buf, k], sem.at[buf], d_ref[TOP_K * r + k], r)

    def drain(buf):
        def body(r, carry):
            for k in range(TOP_K):
                _row_copy(ys_hbm, y_ref.at[buf, k], sem.at[buf], 0, r).wait()
            return carry
        lax.fori_loop(0, bt, body, 0, unroll=8)

    @pl.when(i == 0)
    def _():
        def issue(r, carry):
            for k in range(TOP_K):
                fetch(dest_ref, slot, r, k).start()
            return carry
        lax.fori_loop(0, bt, issue, 0, unroll=8)

    drain(slot)
    out = x_ref[...] + (rg_ref[:, 0:1] * y_ref[slot, 0] + rg_ref[:, 1:2] * y_ref[slot, 1])
    if proj:
        xo_ref[...] = out
    for r in range(bt):
        for k in range(TOP_K):
            fetch(destn_ref, 1 - slot, r, k).start(priority=k)
    h = _rmsnorm(out, g_ref[...])
    if proj:
        z_ref[...] = jnp.dot(h.astype(BF16), w_ref[...], preferred_element_type=F32)
    else:
        o_ref[...] = h

    @pl.when(i == pl.num_programs(0) - 1)
    def _():
        drain(1 - slot)


def _combine(dest, rg, x, ys, g, w=None, *, layer=0, bt):
    t, d = x.shape
    n = t // bt
    proj = w is not None
    in_specs = [
        pl.BlockSpec((TOP_K * bt,), lambda i: (i,), memory_space=pltpu.SMEM),
        pl.BlockSpec((TOP_K * bt,), lambda i: (jnp.minimum(i + 1, n - 1),), memory_space=pltpu.SMEM),
        pl.BlockSpec((bt, V7X_LANES), lambda i: (i, 0)),
        pl.BlockSpec((bt, d), lambda i: (i, 0)),
        pl.BlockSpec(memory_space=pl.ANY),
        pl.BlockSpec((1, d), lambda i: (0, 0)),
    ]
    args = [dest, dest, rg, x, ys, g]
    out_specs = [pl.BlockSpec((bt, d), lambda i: (i, 0))]
    out_shape = [jax.ShapeDtypeStruct((t, d), F32)]
    nbytes = 2 * 2 * bt * d * 4 + 2 * TOP_K * bt * d * 4
    if proj:
        d_in = w.shape[2]
        in_specs.append(pl.BlockSpec((None, d, d_in), lambda i: (layer, 0, 0)))
        args.append(w)
        out_specs.append(pl.BlockSpec((bt, d_in), lambda i: (i, 0)))
        out_shape.append(jax.ShapeDtypeStruct((t, d_in), F32))
        nbytes += 2 * (bt * d_in * 4 + d * d_in * 2)
    res = pl.pallas_call(
        functools.partial(_combine_body, proj=proj),
        grid=(n,),
        in_specs=in_specs,
        out_specs=out_specs,
        out_shape=out_shape,
        scratch_shapes=[pltpu.VMEM((2, TOP_K, bt, d), F32), pltpu.SemaphoreType.DMA((2,))],
        compiler_params=pltpu.CompilerParams(
            dimension_semantics=("arbitrary",), vmem_limit_bytes=_vmem_limit(nbytes + 8 * 2**20)),
        name="combine",
    )(*args)
    return tuple(res) if proj else res[0]


def _routing_tables(ri, cnt, *, n_experts, sb, n_sb):
    t = ri.shape[0]
    experts = ri[:, 0:TOP_K]
    ranks = ri[:, TOP_K:2 * TOP_K]
    counts = cnt[0, 0:n_experts]
    nsb_e = (counts + sb - 1) // sb
    sb_end = jnp.cumsum(nsb_e)
    sb_start = sb_end - nsb_e
    dest = (sb_start * sb)[experts] + ranks
    tok = jnp.broadcast_to(jnp.arange(t, dtype=jnp.int32)[:, None], (t, TOP_K))
    slot_tok = jnp.zeros((n_sb * sb,), jnp.int32).at[dest.reshape(-1)].set(tok.reshape(-1))
    ids = jnp.arange(n_sb, dtype=jnp.int32)
    used = ids < sb_end[-1]
    owner_of = jnp.where(used, ids, jnp.maximum(sb_end[-1] - 1, 0))
    sb_expert = jnp.minimum(jnp.sum(sb_end[None, :] <= owner_of[:, None], axis=1), n_experts - 1).astype(jnp.int32)
    nvalid = jnp.clip(counts[sb_expert] - (ids - sb_start[sb_expert]) * sb, 0, sb)
    sb_nvalid = jnp.where(used, nvalid, 0).astype(jnp.int32)
    return sb_expert, sb_nvalid, slot_tok, dest.reshape(-1).astype(jnp.int32)


def _block(n, pref, mult):
    if n <= pref:
        return n
    b = (pref // mult) * mult
    while n % b:
        b -= mult
    return b


def _forward(x, norm_mix_g, w_in, pool_w, pool_scale, sgu_norm_g, sgu_w, sgu_b, w_out, norm_ffn_g,
             ffn_w_gate, ffn_w_up, ffn_w_down, router_w, moe_w_gate, moe_w_up, moe_w_down,
             final_norm_g, *, bm_mix=512, bm_ffn=512, tf=512, sb=1024, sub=256, half=512, bt=256):
    b, s, d = x.shape
    assert b == 1, "pooling windows and gating chunks are laid out for a single sequence"
    t = b * s
    depth = w_in.shape[0]
    n_experts = router_w.shape[-1]
    n_heads, chunk = sgu_b.shape[1:]
    d_sgu = sgu_norm_g.shape[1]
    assert chunk == SGU_CHUNK and depth % 2 == 0 and n_experts <= V7X_LANES
    bm_mix = _block(t, bm_mix, SGU_CHUNK)
    bm_ffn = _block(t, bm_ffn, V7X_SUBLANES)
    bt = _block(t, bt, V7X_SUBLANES)
    sb = min(sb, t)
    sub = min(sub, sb)
    half = min(half, sb)
    tf = _block(ffn_w_gate.shape[-1], tf, V7X_LANES)
    n_sb = (t * TOP_K + n_experts * (sb - 1)) // sb

    x = x.reshape(t, d)
    row = lambda a: a.reshape(1, -1)
    w_in, pool_w, sgu_w, w_out = (a.astype(BF16) for a in (w_in, pool_w, sgu_w, w_out))
    ffn_w_gate, ffn_w_up, ffn_w_down = (a.astype(BF16) for a in (ffn_w_gate, ffn_w_up, ffn_w_down))
    z = None
    for l in range(depth):
        if z is None:
            z = _in_proj(x, row(norm_mix_g[l]), w_in, layer=l, bm=bm_mix)
        bias = jnp.broadcast_to(sgu_b[l].T[:, :, None], (chunk, n_heads, d_sgu // n_heads)).reshape(chunk, d_sgu)
        x = _mixer(z, x, pool_w, row(pool_scale[l]), row(sgu_norm_g[l]), sgu_w, bias, w_out, layer=l, bm=bm_mix)
        z = None
        i = l // 2
        if l % 2 == 0:
            x = _dense_ffn(x, row(norm_ffn_g[l]), ffn_w_gate, ffn_w_up, ffn_w_down,
                           layer=i, bm=_block(t, sb, half), half=half, tf=tf)
        else:
            rw = jnp.pad(router_w[i], ((0, 0), (0, V7X_LANES - n_experts)))
            h_packed, ri, rg, cnt = _router(x, row(norm_ffn_g[l]), rw, bm=bm_ffn, n_experts=n_experts)
            sb_expert, sb_nvalid, slot_tok, dest = _routing_tables(ri, cnt, n_experts=n_experts, sb=sb, n_sb=n_sb)
            ys = _experts(sb_expert, sb_nvalid, slot_tok, h_packed, moe_w_gate, moe_w_up, moe_w_down,
                          layer=i, sb=sb, sub=sub, half=half, tf=tf)
            if l == depth - 1:
                x = _combine(dest, rg, x, ys, row(final_norm_g), bt=bt)
            else:
                x, z = _combine(dest, rg, x, ys, row(norm_mix_g[l + 1]), w_in, layer=l + 1, bt=bt)
    return x.reshape(b, s, d)


def kernel(x, norm_mix_g, w_in, pool_w, pool_scale, sgu_norm_g, sgu_w, sgu_b, w_out, norm_ffn_g,
           ffn_w_gate, ffn_w_up, ffn_w_down, router_w, moe_w_gate, moe_w_up, moe_w_down, final_norm_g):
    return _forward(x, norm_mix_g, w_in, pool_w, pool_scale, sgu_norm_g, sgu_w, sgu_b, w_out, norm_ffn_g,
                    ffn_w_gate, ffn_w_up, ffn_w_down, router_w, moe_w_gate, moe_w_up, moe_w_down,
                    final_norm_g)
```

```python
import functools

import jax
import jax.numpy as jnp
from jax import lax
from jax.experimental import pallas as pl
from jax.experimental.pallas import tpu as pltpu

RMS_EPS = 1e-6
LN_EPS = 1e-5
POOL_WINDOWS = (2, 4, 8, 16)
SGU_CHUNK = 128
TOP_K = 2

V7X_LANES = 128
V7X_SUBLANES = 8
V7X_VMEM_BYTES = 64 * 1024 * 1024
HALO = max(POOL_WINDOWS) // 2
assert HALO == V7X_SUBLANES

BF16 = jnp.bfloat16
F32 = jnp.float32


def _vmem_limit(nbytes):
    return int(min(nbytes, V7X_VMEM_BYTES - 6 * 1024 * 1024))


def _rmsnorm(x, g):
    return x * lax.rsqrt(jnp.mean(x * x, axis=-1, keepdims=True) + RMS_EPS) * g


def _gelu(x):
    return 0.5 * x * (1.0 + lax.erf(x * (2.0 ** -0.5)))


def _silu(x):
    return x / (1.0 + jnp.exp(-x))


def _in_proj_body(x_ref, g_ref, w_ref, z_ref):
    h = _rmsnorm(x_ref[...], g_ref[...]).astype(BF16)
    z_ref[...] = jnp.dot(h, w_ref[...], preferred_element_type=F32)


def _in_proj(x, g, w, *, layer, bm):
    t, d = x.shape
    d_in = w.shape[2]
    return pl.pallas_call(
        _in_proj_body,
        grid=(t // bm,),
        in_specs=[
            pl.BlockSpec((bm, d), lambda i: (i, 0)),
            pl.BlockSpec((1, d), lambda i: (0, 0)),
            pl.BlockSpec((None, d, d_in), lambda i: (layer, 0, 0)),
        ],
        out_specs=pl.BlockSpec((bm, d_in), lambda i: (i, 0)),
        out_shape=jax.ShapeDtypeStruct((t, d_in), F32),
        compiler_params=pltpu.CompilerParams(
            dimension_semantics=("arbitrary",),
            vmem_limit_bytes=_vmem_limit(2 * (bm * d * 4 + d * d_in * 2 + bm * d_in * 4) + 16 * 2**20)),
        name="in_proj",
    )(x, g, w)


def _mixer_body(z_ref, zp_ref, zn_ref, x_ref, pw_ref, ps_ref, ng_ref, sw_ref, sb_ref, wo_ref,
                o_ref, ext_ref, vn_ref, y_ref, *, seq):
    i = pl.program_id(0)
    bm = z_ref.shape[0]
    d_pool = ps_ref.shape[1]
    d_sgu = ng_ref.shape[1]
    pg = d_pool // len(POOL_WINDOWS)
    n_heads = sw_ref.shape[0]
    hd = d_sgu // n_heads

    ext_ref[0:HALO, :] = jnp.where(i > 0, zp_ref[...], 0.0)
    ext_ref[HALO:HALO + bm, :] = z_ref[:, 0:d_pool]
    ext_ref[HALO + bm:2 * HALO + bm, :] = jnp.where(i < pl.num_programs(0) - 1, zn_ref[...], 0.0)
    t = i * bm + lax.broadcasted_iota(jnp.int32, (bm, 1), 0)
    for g, w in enumerate(POOL_WINDOWS):
        cs = slice(g * pg, (g + 1) * pg)
        first = HALO - w // 2
        acc = ext_ref[first:first + bm, cs]
        for j in range(1, w):
            acc = acc + ext_ref[first + j:first + j + bm, cs]
        cnt = jnp.minimum(t + (w - w // 2), seq) - jnp.maximum(t - w // 2, 0)
        diff = acc * (1.0 / cnt.astype(F32)) - z_ref[:, cs]
        yp = jnp.dot(diff.astype(BF16), pw_ref[g], preferred_element_type=F32) * ps_ref[:, cs]
        y_ref[:, cs] = yp.astype(BF16)

    v = _gelu(z_ref[:, d_pool + d_sgu:d_pool + 2 * d_sgu])
    mu = jnp.mean(v, axis=-1, keepdims=True)
    vc = v - mu
    var = jnp.mean(vc * vc, axis=-1, keepdims=True)
    vn_ref[...] = (vc * lax.rsqrt(var + LN_EPS) * ng_ref[...]).astype(BF16)
    for c in range(bm // SGU_CHUNK):
        rs = slice(c * SGU_CHUNK, (c + 1) * SGU_CHUNK)
        for h in range(n_heads):
            hs = slice(h * hd, (h + 1) * hd)
            mixed = jnp.dot(sw_ref[h], vn_ref[rs, hs], preferred_element_type=F32) + sb_ref[:, hs]
            u = _gelu(z_ref[rs, d_pool + h * hd:d_pool + (h + 1) * hd])
            y_ref[rs, d_pool + h * hd:d_pool + (h + 1) * hd] = (u * mixed).astype(BF16)

    o_ref[...] = x_ref[...] + jnp.dot(y_ref[...], wo_ref[...], preferred_element_type=F32)


def _mixer(z, x, pool_w, pool_scale, norm_g, sgu_w, sgu_bias, w_out, *, layer, bm):
    t, d = x.shape
    d_in = z.shape[1]
    d_pool = pool_scale.shape[1]
    d_sgu = norm_g.shape[1]
    d_mix = d_pool + d_sgu
    nh = t // HALO
    const2 = lambda i: (0, 0)
    return pl.pallas_call(
        functools.partial(_mixer_body, seq=t),
        grid=(t // bm,),
        in_specs=[
            pl.BlockSpec((bm, d_in), lambda i: (i, 0)),
            pl.BlockSpec((HALO, d_pool), lambda i: (jnp.maximum(i * (bm // HALO) - 1, 0), 0)),
            pl.BlockSpec((HALO, d_pool), lambda i: (jnp.minimum((i + 1) * (bm // HALO), nh - 1), 0)),
            pl.BlockSpec((bm, d), lambda i: (i, 0)),
            pl.BlockSpec((None,) + pool_w.shape[1:], lambda i: (layer, 0, 0, 0)),
            pl.BlockSpec((1, d_pool), const2),
            pl.BlockSpec((1, d_sgu), const2),
            pl.BlockSpec((None,) + sgu_w.shape[1:], lambda i: (layer, 0, 0, 0)),
            pl.BlockSpec(sgu_bias.shape, const2),
            pl.BlockSpec((None, d_mix, d), lambda i: (layer, 0, 0)),
        ],
        out_specs=pl.BlockSpec((bm, d), lambda i: (i, 0)),
        out_shape=jax.ShapeDtypeStruct((t, d), F32),
        scratch_shapes=[
            pltpu.VMEM((bm + 2 * HALO, d_pool), F32),
            pltpu.VMEM((bm, d_sgu), BF16),
            pltpu.VMEM((bm, d_mix), BF16),
        ],
        compiler_params=pltpu.CompilerParams(
            dimension_semantics=("arbitrary",),
            vmem_limit_bytes=_vmem_limit(2 * (bm * d_in * 4 + 2 * bm * d * 4 + d_mix * d * 2) + 20 * 2**20)),
        name="mixer",
    )(z, z, z, x, pool_w, pool_scale, norm_g, sgu_w, sgu_bias, w_out)


def _dense_ffn_body(x_ref, g_ref, wg_ref, wu_ref, wd_ref, o_ref, h_ref, *, half):
    @pl.when(pl.program_id(1) == 0)
    def _():
        x = x_ref[...]
        h_ref[...] = _rmsnorm(x, g_ref[...]).astype(BF16)
        o_ref[...] = x

    for j in range(h_ref.shape[0] // half):
        rows = pl.ds(j * half, half)
        h = h_ref[rows, :]
        gate = jnp.dot(h, wg_ref[...], preferred_element_type=F32)
        up = jnp.dot(h, wu_ref[...], preferred_element_type=F32)
        act = (_silu(gate) * up).astype(BF16)
        o_ref[rows, :] += jnp.dot(act, wd_ref[...], preferred_element_type=F32)


def _dense_ffn(x, g, w_gate, w_up, w_down, *, layer, bm, half, tf):
    t, d = x.shape
    d_ff = w_gate.shape[2]
    return pl.pallas_call(
        functools.partial(_dense_ffn_body, half=half),
        grid=(t // bm, d_ff // tf),
        in_specs=[
            pl.BlockSpec((bm, d), lambda i, f: (i, 0)),
            pl.BlockSpec((1, d), lambda i, f: (0, 0)),
            pl.BlockSpec((None, d, tf), lambda i, f: (layer, 0, f)),
            pl.BlockSpec((None, d, tf), lambda i, f: (layer, 0, f)),
            pl.BlockSpec((None, tf, d), lambda i, f: (layer, f, 0)),
        ],
        out_specs=pl.BlockSpec((bm, d), lambda i, f: (i, 0)),
        out_shape=jax.ShapeDtypeStruct((t, d), F32),
        scratch_shapes=[pltpu.VMEM((bm, d), BF16)],
        compiler_params=pltpu.CompilerParams(
            dimension_semantics=("arbitrary", "arbitrary"),
            vmem_limit_bytes=_vmem_limit(2 * (2 * bm * d * 4 + 3 * d * tf * 2) + bm * d * 2
                                         + 3 * half * tf * 4 + 8 * 2**20)),
        name="dense_ffn",
    )(x, g, w_gate, w_up, w_down)


def _router_body(x_ref, g_ref, rw_ref, hp_ref, ri_ref, rg_ref, cnt_ref, run_ref, *, n_experts):
    i = pl.program_id(0)
    bm, d = x_ref.shape

    @pl.when(i == 0)
    def _():
        run_ref[...] = jnp.zeros_like(run_ref)

    h = _rmsnorm(x_ref[...], g_ref[...])

    h_hi = h.astype(BF16)
    bits = pltpu.bitcast(h_hi.astype(F32), jnp.uint32)
    hp_ref[...] = (bits[:, 0:d // 2] >> 16) | (bits[:, d // 2:d] & jnp.uint32(0xFFFF0000))

    h_lo = (h - h_hi.astype(F32)).astype(BF16)
    rw = rw_ref[...]
    rw_hi = rw.astype(BF16)
    rw_lo = (rw - rw_hi.astype(F32)).astype(BF16)
    logits = (jnp.dot(h_hi, rw_hi, preferred_element_type=F32)
              + (jnp.dot(h_lo, rw_hi, preferred_element_type=F32) + jnp.dot(h_hi, rw_lo, preferred_element_type=F32)))
    lane = lax.broadcasted_iota(jnp.int32, (bm, V7X_LANES), 1)
    neg = jnp.float32(-jnp.inf)
    lg = jnp.where(lane < n_experts, logits, neg)
    m1 = jnp.max(lg, axis=-1, keepdims=True)
    e1 = jnp.min(jnp.where(lg == m1, lane, V7X_LANES), axis=-1, keepdims=True)
    lg2 = jnp.where(lane == e1, neg, lg)
    m2 = jnp.max(lg2, axis=-1, keepdims=True)
    e2 = jnp.min(jnp.where(lg2 == m2, lane, V7X_LANES), axis=-1, keepdims=True)
    ex = jnp.exp(m2 - m1)
    g1 = 1.0 / (1.0 + ex)
    g2 = ex / (1.0 + ex)

    sel = jnp.where((lane == e1) | (lane == e2), 1.0, 0.0)
    r = lax.broadcasted_iota(jnp.int32, (bm, bm), 0)
    c = lax.broadcasted_iota(jnp.int32, (bm, bm), 1)
    tri = jnp.where(c < r, 1.0, 0.0).astype(BF16)
    prefix = jnp.dot(tri, sel.astype(BF16), preferred_element_type=F32) + run_ref[0:1, :]
    rank1 = jnp.sum(jnp.where(lane == e1, prefix, 0.0), axis=-1, keepdims=True).astype(jnp.int32)
    rank2 = jnp.sum(jnp.where(lane == e2, prefix, 0.0), axis=-1, keepdims=True).astype(jnp.int32)
    run_ref[0:1, :] = run_ref[0:1, :] + jnp.sum(sel, axis=0, keepdims=True)

    ri_ref[...] = jnp.where(lane == 0, e1, jnp.where(lane == 1, e2,
                            jnp.where(lane == 2, rank1, jnp.where(lane == 3, rank2, 0))))
    rg_ref[...] = jnp.where(lane == 0, g1, jnp.where(lane == 1, g2, 0.0))
    cnt_ref[...] = run_ref[...].astype(jnp.int32)


def _router(x, g, router_w_padded, *, bm, n_experts):
    t, d = x.shape
    return pl.pallas_call(
        functools.partial(_router_body, n_experts=n_experts),
        grid=(t // bm,),
        in_specs=[
            pl.BlockSpec((bm, d), lambda i: (i, 0)),
            pl.BlockSpec((1, d), lambda i: (0, 0)),
            pl.BlockSpec((d, V7X_LANES), lambda i: (0, 0)),
        ],
        out_specs=[
            pl.BlockSpec((bm, d // 2), lambda i: (i, 0)),
            pl.BlockSpec((bm, V7X_LANES), lambda i: (i, 0)),
            pl.BlockSpec((bm, V7X_LANES), lambda i: (i, 0)),
            pl.BlockSpec((V7X_SUBLANES, V7X_LANES), lambda i: (0, 0)),
        ],
        out_shape=[
            jax.ShapeDtypeStruct((t, d // 2), jnp.uint32),
            jax.ShapeDtypeStruct((t, V7X_LANES), jnp.int32),
            jax.ShapeDtypeStruct((t, V7X_LANES), F32),
            jax.ShapeDtypeStruct((V7X_SUBLANES, V7X_LANES), jnp.int32),
        ],
        scratch_shapes=[pltpu.VMEM((V7X_SUBLANES, V7X_LANES), F32)],
        compiler_params=pltpu.CompilerParams(dimension_semantics=("arbitrary",)),
        name="router",
    )(x, g, router_w_padded)


def _row_copy(src_hbm, dst_ref, sem, src_row, dst_row):
    return pltpu.make_async_copy(src_hbm.at[pl.ds(src_row, 1)], dst_ref.at[pl.ds(dst_row, 1)], sem)


def _expert_body(exp_ref, nv_ref, tok_ref, tokn_ref, hp_hbm, wg_ref, wu_ref, wd_ref, ys_hbm,
                 acc_ref, xw_ref, xs_ref, wgb_ref, wub_ref, wdb_ref, gsem, wsem, *, sub, half, rpf, n_gather):
    s = pl.program_id(0)
    f = pl.program_id(1)
    ns = pl.num_programs(0)
    nf = pl.num_programs(1)
    sb, d = acc_ref.shape
    nsub = sb // sub
    nvalid = nv_ref[s]
    slot = s % 2

    def writeback(blk):
        return pltpu.make_async_copy(acc_ref, ys_hbm.at[pl.ds(blk * sb, sb)], wsem)

    def gather(tokens_ref, buf, r):
        return _row_copy(hp_hbm, xw_ref.at[buf], gsem.at[buf], tokens_ref[jnp.minimum(r, sb - 1)], r)

    def drain(buf):
        def body(r, carry):
            _row_copy(hp_hbm, xw_ref.at[buf], gsem.at[buf], 0, r).wait()
            return carry
        lax.fori_loop(0, n_gather, body, 0, unroll=8)

    @pl.when(f == 0)
    def _():
        @pl.when(s == 0)
        def _():
            def issue(r, carry):
                gather(tok_ref, slot, r).start()
                return carry
            lax.fori_loop(0, n_gather, issue, 0, unroll=8)

        @pl.when(s > 0)
        def _():
            writeback(s - 1).wait()
        acc_ref[...] = jnp.zeros_like(acc_ref)

        drain(slot)
        for j in range(nsub):
            rows = pl.ds(j * sub, sub)

            @pl.when(j * sub < nvalid)
            def _():
                w = xw_ref[slot, rows, :]
                xs_ref[rows, 0:d // 2] = pltpu.bitcast(w << 16, F32).astype(BF16)
                xs_ref[rows, d // 2:d] = pltpu.bitcast(w & jnp.uint32(0xFFFF0000), F32).astype(BF16)

    def cast_weights():
        wgb_ref[...] = wg_ref[...].astype(BF16)
        wub_ref[...] = wu_ref[...].astype(BF16)
        wdb_ref[...] = wd_ref[...].astype(BF16)

    def ffn(rows):
        x = xs_ref[rows, :]
        gate = jnp.dot(x, wgb_ref[...], preferred_element_type=F32)
        up = jnp.dot(x, wub_ref[...], preferred_element_type=F32)
        act = (_silu(gate) * up).astype(BF16)
        acc_ref[rows, :] += jnp.dot(act, wdb_ref[...], preferred_element_type=F32)

    full = nvalid > (nsub - 1) * sub
    n_halves = sb // half

    @pl.when(full)
    def _():
        cast_weights()
        for j in range(n_halves):
            if j == n_halves - 1:
                for k in range(rpf):
                    gather(tokn_ref, 1 - slot, f * rpf + k).start(priority=1)
            ffn(pl.ds(j * half, half))

    @pl.when(jnp.logical_and(nvalid > 0, jnp.logical_not(full)))
    def _():
        cast_weights()
        for j in range(nsub - 1):
            @pl.when(j * sub < nvalid)
            def _():
                if j == 0:
                    for k in range(rpf):
                        gather(tokn_ref, 1 - slot, f * rpf + k).start(priority=1)
                ffn(pl.ds(j * sub, sub))

    @pl.when(nvalid == 0)
    def _():
        def issue(k, carry):
            gather(tokn_ref, 1 - slot, f * rpf + k).start(priority=1)
            return carry
        lax.fori_loop(0, rpf, issue, 0, unroll=8)

    @pl.when(f == nf - 1)
    def _():
        writeback(s).start()

        @pl.when(s == ns - 1)
        def _():
            writeback(s).wait()
            drain(1 - slot)


def _experts(sb_expert, sb_nvalid, slot_tok, h_packed, w_gate, w_up, w_down, *, layer, sb, sub, half, tf):
    n_sb = sb_expert.shape[0]
    _, _, d, d_ff = w_gate.shape
    nf = d_ff // tf
    rpf = pl.cdiv(sb, nf)
    n_gather = rpf * nf

    def f_idx(s, f, nv_ref):
        return jnp.where(nv_ref[s] > 0, f, nf - 1)

    grid_spec = pltpu.PrefetchScalarGridSpec(
        num_scalar_prefetch=2,
        grid=(n_sb, nf),
        in_specs=[
            pl.BlockSpec((sb,), lambda s, f, e, nv: (s,), memory_space=pltpu.SMEM),
            pl.BlockSpec((sb,), lambda s, f, e, nv: (jnp.minimum(s + 1, n_sb - 1),), memory_space=pltpu.SMEM),
            pl.BlockSpec(memory_space=pl.ANY),
            pl.BlockSpec((None, None, d, tf), lambda s, f, e, nv: (layer, e[s], 0, f_idx(s, f, nv))),
            pl.BlockSpec((None, None, d, tf), lambda s, f, e, nv: (layer, e[s], 0, f_idx(s, f, nv))),
            pl.BlockSpec((None, None, tf, d), lambda s, f, e, nv: (layer, e[s], f_idx(s, f, nv), 0)),
        ],
        out_specs=pl.BlockSpec(memory_space=pl.ANY),
        scratch_shapes=[
            pltpu.VMEM((sb, d), F32),
            pltpu.VMEM((2, pl.cdiv(n_gather, V7X_SUBLANES) * V7X_SUBLANES, d // 2), jnp.uint32),
            pltpu.VMEM((sb, d), BF16),
            pltpu.VMEM((d, tf), BF16),
            pltpu.VMEM((d, tf), BF16),
            pltpu.VMEM((tf, d), BF16),
            pltpu.SemaphoreType.DMA((2,)),
            pltpu.SemaphoreType.DMA,
        ],
    )
    weight_bytes = 3 * d * tf * 4
    return pl.pallas_call(
        functools.partial(_expert_body, sub=sub, half=half, rpf=rpf, n_gather=n_gather),
        grid_spec=grid_spec,
        out_shape=jax.ShapeDtypeStruct((n_sb * sb, d), F32),
        compiler_params=pltpu.CompilerParams(
            dimension_semantics=("arbitrary", "arbitrary"),
            vmem_limit_bytes=_vmem_limit(2 * weight_bytes + weight_bytes // 2 + sb * d * (4 + 2)
                                         + 2 * n_gather * d * 2 + 3 * half * tf * 4 + 8 * 2**20)),
        name="experts",
    )(sb_expert, sb_nvalid, slot_tok, slot_tok, h_packed, w_gate, w_up, w_down)


def _combine_body(dest_ref, destn_ref, rg_ref, x_ref, ys_hbm, g_ref, *rest, proj):
    if proj:
        w_ref, xo_ref, z_ref, y_ref, sem = rest
    else:
        o_ref, y_ref, sem = rest
    i = pl.program_id(0)
    bt = x_ref.shape[0]
    slot = i % 2

    def fetch(d_ref, buf, r, k):
        return _row_copy(ys_hbm, y_ref.at[buf, k], sem.at[buf], d_ref[TOP_K * r + k], r)

    def drain(buf):
        def body(r, carry):
            for k in range(TOP_K):
                _row_copy(ys_hbm, y_ref.at[buf, k], sem.at[buf], 0, r).wait()
            return carry
        lax.fori_loop(0, bt, body, 0, unroll=8)

    @pl.when(i == 0)
    def _():
        def issue(r, carry):
            for k in range(TOP_K):
                fetch(dest_ref, slot, r, k).start()
            return carry
        lax.fori_loop(0, bt, issue, 0, unroll=8)

    drain(slot)
    out = x_ref[...] + (rg_ref[:, 0:1] * y_ref[slot, 0] + rg_ref[:, 1:2] * y_ref[slot, 1])
    if proj:
        xo_ref[...] = out
    for r in range(bt):
        for k in range(TOP_K):
            fetch(destn_ref, 1 - slot, r, k).start(priority=k)
    h = _rmsnorm(out, g_ref[...])
    if proj:
        z_ref[...] = jnp.dot(h.astype(BF16), w_ref[...], preferred_element_type=F32)
    else:
        o_ref[...] = h

    @pl.when(i == pl.num_programs(0) - 1)
    def _():
        drain(1 - slot)


def _combine(dest, rg, x, ys, g, w=None, *, layer=0, bt):
    t, d = x.shape
    n = t // bt
    proj = w is not None
    in_specs = [
        pl.BlockSpec((TOP_K * bt,), lambda i: (i,), memory_space=pltpu.SMEM),
        pl.BlockSpec((TOP_K * bt,), lambda i: (jnp.minimum(i + 1, n - 1),), memory_space=pltpu.SMEM),
        pl.BlockSpec((bt, V7X_LANES), lambda i: (i, 0)),
        pl.BlockSpec((bt, d), lambda i: (i, 0)),
        pl.BlockSpec(memory_space=pl.ANY),
        pl.BlockSpec((1, d), lambda i: (0, 0)),
    ]
    args = [dest, dest, rg, x, ys, g]
    out_specs = [pl.BlockSpec((bt, d), lambda i: (i, 0))]
    out_shape = [jax.ShapeDtypeStruct((t, d), F32)]
    nbytes = 2 * 2 * bt * d * 4 + 2 * TOP_K * bt * d * 4
    if proj:
        d_in = w.shape[2]
        in_specs.append(pl.BlockSpec((None, d, d_in), lambda i: (layer, 0, 0)))
        args.append(w)
        out_specs.append(pl.BlockSpec((bt, d_in), lambda i: (i, 0)))
        out_shape.append(jax.ShapeDtypeStruct((t, d_in), F32))
        nbytes += 2 * (bt * d_in * 4 + d * d_in * 2)
    res = pl.pallas_call(
        functools.partial(_combine_body, proj=proj),
        grid=(n,),
        in_specs=in_specs,
        out_specs=out_specs,
        out_shape=out_shape,
        scratch_shapes=[pltpu.VMEM((2, TOP_K, bt, d), F32), pltpu.SemaphoreType.DMA((2,))],
        compiler_params=pltpu.CompilerParams(
            dimension_semantics=("arbitrary",), vmem_limit_bytes=_vmem_limit(nbytes + 8 * 2**20)),
        name="combine",
    )(*args)
    return tuple(res) if proj else res[0]


def _routing_tables(ri, cnt, *, n_experts, sb, n_sb):
    t = ri.shape[0]
    experts = ri[:, 0:TOP_K]
    ranks = ri[:, TOP_K:2 * TOP_K]
    counts = cnt[0, 0:n_experts]
    nsb_e = (counts + sb - 1) // sb
    sb_end = jnp.cumsum(nsb_e)
    sb_start = sb_end - nsb_e
    dest = (sb_start * sb)[experts] + ranks
    tok = jnp.broadcast_to(jnp.arange(t, dtype=jnp.int32)[:, None], (t, TOP_K))
    slot_tok = jnp.zeros((n_sb * sb,), jnp.int32).at[dest.reshape(-1)].set(tok.reshape(-1))
    ids = jnp.arange(n_sb, dtype=jnp.int32)
    used = ids < sb_end[-1]
    owner_of = jnp.where(used, ids, jnp.maximum(sb_end[-1] - 1, 0))
    sb_expert = jnp.minimum(jnp.sum(sb_end[None, :] <= owner_of[:, None], axis=1), n_experts - 1).astype(jnp.int32)
    nvalid = jnp.clip(counts[sb_expert] - (ids - sb_start[sb_expert]) * sb, 0, sb)
    sb_nvalid = jnp.where(used, nvalid, 0).astype(jnp.int32)
    return sb_expert, sb_nvalid, slot_tok, dest.reshape(-1).astype(jnp.int32)


def _block(n, pref, mult):
    if n <= pref:
        return n
    b = (pref // mult) * mult
    while n % b:
        b -= mult
    return b


def _forward(x, norm_mix_g, w_in, pool_w, pool_scale, sgu_norm_g, sgu_w, sgu_b, w_out, norm_ffn_g,
             ffn_w_gate, ffn_w_up, ffn_w_down, router_w, moe_w_gate, moe_w_up, moe_w_down,
             final_norm_g, *, bm_mix=512, bm_ffn=512, tf=512, sb=1024, sub=256, half=512, bt=256):
    b, s, d = x.shape
    assert b == 1, "pooling windows and gating chunks are laid out for a single sequence"
    t = b * s
    depth = w_in.shape[0]
    n_experts = router_w.shape[-1]
    n_heads, chunk = sgu_b.shape[1:]
    d_sgu = sgu_norm_g.shape[1]
    assert chunk == SGU_CHUNK and depth % 2 == 0 and n_experts <= V7X_LANES
    bm_mix = _block(t, bm_mix, SGU_CHUNK)
    bm_ffn = _block(t, bm_ffn, V7X_SUBLANES)
    bt = _block(t, bt, V7X_SUBLANES)
    sb = min(sb, t)
    sub = min(sub, sb)
    half = min(half, sb)
    tf = _block(ffn_w_gate.shape[-1], tf, V7X_LANES)
    n_sb = (t * TOP_K + n_experts * (sb - 1)) // sb

    x = x.reshape(t, d)
    row = lambda a: a.reshape(1, -1)
    w_in, pool_w, sgu_w, w_out = (a.astype(BF16) for a in (w_in, pool_w, sgu_w, w_out))
    ffn_w_gate, ffn_w_up, ffn_w_down = (a.astype(BF16) for a in (ffn_w_gate, ffn_w_up, ffn_w_down))
    z = None
    for l in range(depth):
        if z is None:
            z = _in_proj(x, row(norm_mix_g[l]), w_in, layer=l, bm=bm_mix)
        bias = jnp.broadcast_to(sgu_b[l].T[:, :, None], (chunk, n_heads, d_sgu // n_heads)).reshape(chunk, d_sgu)
        x = _mixer(z, x, pool_w, row(pool_scale[l]), row(sgu_norm_g[l]), sgu_w, bias, w_out, layer=l, bm=bm_mix // 2)
        z = None
        i = l // 2
        if l % 2 == 0:
            x = _dense_ffn(x, row(norm_ffn_g[l]), ffn_w_gate, ffn_w_up, ffn_w_down,
                           layer=i, bm=_block(t, sb, half), half=half, tf=tf)
        else:
            rw = jnp.pad(router_w[i], ((0, 0), (0, V7X_LANES - n_experts)))
            h_packed, ri, rg, cnt = _router(x, row(norm_ffn_g[l]), rw, bm=2 * bm_ffn, n_experts=n_experts)
            sb_expert, sb_nvalid, slot_tok, dest = _routing_tables(ri, cnt, n_experts=n_experts, sb=sb, n_sb=n_sb)
            ys = _experts(sb_expert, sb_nvalid, slot_tok, h_packed, moe_w_gate, moe_w_up, moe_w_down,
                          layer=i, sb=sb, sub=sub, half=half, tf=tf)
            if l == depth - 1:
                x = _combine(dest, rg, x, ys, row(final_norm_g), bt=bt)
            else:
                x, z = _combine(dest, rg, x, ys, row(norm_mix_g[l + 1]), w_in, layer=l + 1, bt=bt)
    return x.reshape(b, s, d)


def kernel(x, norm_mix_g, w_in, pool_w, pool_scale, sgu_norm_g, sgu_w, sgu_b, w_out, norm_ffn_g,
           ffn_w_gate, ffn_w_up, ffn_w_down, router_w, moe_w_gate, moe_w_up, moe_w_down, final_norm_g):
    return _forward(x, norm_mix_g, w_in, pool_w, pool_scale, sgu_norm_g, sgu_w, sgu_b, w_out, norm_ffn_g,
                    ffn_w_gate, ffn_w_up, ffn_w_down, router_w, moe_w_gate, moe_w_up, moe_w_down,
                    final_norm_g)
```
